```python
import math
import jax, jax.numpy as jnp
from jax import lax
import numpy as np

D_MODEL = 4096
BATCH = 4
SEQ = 4096
DEPTH = 2

N_HEADS = 32
HEAD_DIM = 128
N_KV_GROUPS = 4
HEADS_PER_GROUP = N_HEADS // N_KV_GROUPS
ATTN_WIDTH = N_HEADS * HEAD_DIM
KV_WIDTH = N_KV_GROUPS * HEAD_DIM
CMP_BLOCK = 32
CMP_STRIDE = 16
SEL_BLOCK = 64
SEL_TOPK = 16
WINDOW = 512
Q_CHUNK = 32
CONV_CH = D_MODEL
CONV_WIDTH = 31
D_FF = -(-8 * D_MODEL // (3 * 256)) * 256
NUM_BUCKETS = 32
MAX_EXACT = NUM_BUCKETS // 2
MAX_DISTANCE = 128
N_BRANCH_GATES = 3 * N_HEADS
IN_COLS = ATTN_WIDTH + 6 * KV_WIDTH + 2 * CONV_CH + 2 * D_MODEL + N_BRANCH_GATES
EPS = 1e-6
NEG_INF = -1e30

kernel_name = 'hybrid_nsa_conformer_griffin_merge'


def _rmsnorm(x, g):
    xf = x.astype(jnp.float32)
    y = xf * lax.rsqrt(jnp.mean(xf * xf, axis=-1, keepdims=True) + EPS) * g.astype(jnp.float32)
    return y.astype(x.dtype)


def _t5_bucket(dist):
    n = jnp.maximum(dist, 0)
    nf = jnp.maximum(n, MAX_EXACT).astype(jnp.float32)
    large = MAX_EXACT + (jnp.log(nf / MAX_EXACT) / math.log(MAX_DISTANCE / MAX_EXACT)
                         * (NUM_BUCKETS - MAX_EXACT)).astype(jnp.int32)
    large = jnp.minimum(large, NUM_BUCKETS - 1)
    return jnp.where(n < MAX_EXACT, n, large)


def _masked_softmax(s, mask, axis=-1):
    p = jax.nn.softmax(jnp.where(mask, s, NEG_INF), axis=axis)
    return jnp.where(mask, p, 0.0)


def _cmp_to_sel_matrix(seq):
    ratio = CMP_BLOCK // CMP_STRIDE
    n_cmp = seq // CMP_STRIDE - ratio + 1
    nsb = seq // SEL_BLOCK
    c_start = np.arange(n_cmp) * CMP_STRIDE
    s_start = np.arange(nsb) * SEL_BLOCK
    ov = np.minimum(c_start[:, None] + CMP_BLOCK, s_start[None, :] + SEL_BLOCK) - np.maximum(c_start[:, None], s_start[None, :])
    ov = np.clip(ov, 0, None) // CMP_STRIDE
    return jnp.asarray(ov.astype(np.float32))


def _compress(k, pos, w1, w2):
    B, G, T, dh = k.shape
    ratio = CMP_BLOCK // CMP_STRIDE
    ns = T // CMP_STRIDE
    n_cmp = ns - ratio + 1
    kr = k.reshape(B, G, ns, CMP_STRIDE, dh)
    blocks = jnp.concatenate([kr[:, :, i:i + n_cmp] for i in range(ratio)], axis=3)
    blocks = (blocks + pos).reshape(B, G, n_cmp, CMP_BLOCK * dh)
    return jax.nn.gelu(blocks @ w1) @ w2


def _nsa(q, kv, branch_gates, cmp_pos, cmp_w1, cmp_w2, rel_bias):
    B, T, _ = q.shape
    G, R, dh = N_KV_GROUPS, HEADS_PER_GROUP, HEAD_DIM
    qh = (q * (HEAD_DIM ** -0.5)).reshape(B, T, G, R, dh).transpose(0, 2, 3, 1, 4)
    k_cmp_in, v_cmp_in, k_slc, v_slc, k_win, v_win = [
        a.reshape(B, T, G, dh).transpose(0, 2, 1, 3) for a in jnp.split(kv, 6, axis=-1)]
    k_cmp = _compress(k_cmp_in, cmp_pos[0], cmp_w1[0], cmp_w2[0])
    v_cmp = _compress(v_cmp_in, cmp_pos[1], cmp_w1[1], cmp_w2[1])
    n_cmp = k_cmp.shape[2]
    cmp_end = jnp.arange(n_cmp, dtype=jnp.int32) * CMP_STRIDE + CMP_BLOCK - 1
    nsb = T // SEL_BLOCK
    n_sel = min(SEL_TOPK, nsb)
    sel_map = _cmp_to_sel_matrix(T)
    k_blocks = k_slc.reshape(B, G, nsb, SEL_BLOCK, dh)
    v_blocks = v_slc.reshape(B, G, nsb, SEL_BLOCK, dh)
    pad = ((0, 0), (0, 0), (WINDOW, 0), (0, 0))
    k_win_pad = jnp.pad(k_win, pad)
    v_win_pad = jnp.pad(v_win, pad)
    gates = jax.nn.sigmoid(branch_gates.astype(jnp.float32)).reshape(B, T, G, R, 3).transpose(0, 2, 3, 1, 4)
    table = rel_bias.astype(jnp.float32).T.reshape(G, R, NUM_BUCKETS)
    b_ix = jnp.arange(B)[:, None, None, None]
    g_ix = jnp.arange(G)[None, :, None, None]
    g6 = jnp.arange(G)[None, :, None, None, None, None]
    r6 = jnp.arange(R)[None, None, :, None, None, None]
    blk = jnp.arange(nsb, dtype=jnp.int32)

    def chunk(c):
        t0 = c * Q_CHUNK
        qc = lax.dynamic_slice_in_dim(qh, t0, Q_CHUNK, axis=3)
        gc = lax.dynamic_slice_in_dim(gates, t0, Q_CHUNK, axis=3)
        tpos = t0 + jnp.arange(Q_CHUNK, dtype=jnp.int32)
        d_c = tpos[:, None] - cmp_end[None, :]
        s = jnp.einsum('bgrqd,bgnd->bgrqn', qc, k_cmp).astype(jnp.float32) + table[:, :, _t5_bucket(d_c)]
        p_cmp = _masked_softmax(s, d_c >= 0)
        o_cmp = jnp.einsum('bgrqn,bgnd->bgrqd', p_cmp.astype(v_cmp.dtype), v_cmp)
        imp = jnp.einsum('bgrqn,nj->bgqj', p_cmp, sel_map)
        cur = tpos // SEL_BLOCK
        forced = (blk[None, :] == 0) | (blk[None, :] == cur[:, None]) | (blk[None, :] == cur[:, None] - 1)
        valid = blk[None, :] <= cur[:, None]
        imp = jnp.where(valid, jnp.where(forced, jnp.inf, imp), -jnp.inf)
        _, idx = lax.top_k(imp, n_sel)
        blk_valid = idx <= cur[None, None, :, None]
        kb = k_blocks[b_ix, g_ix, idx]
        vb = v_blocks[b_ix, g_ix, idx]
        kpos = idx[..., None] * SEL_BLOCK + jnp.arange(SEL_BLOCK, dtype=jnp.int32)
        d_s = tpos[None, None, :, None, None] - kpos
        m_s = ((d_s >= 0) & blk_valid[..., None])[:, :, None]
        s = jnp.einsum('bgrqd,bgqkld->bgrqkl', qc, kb).astype(jnp.float32) + table[g6, r6, _t5_bucket(d_s)[:, :, None]]
        p_s = _masked_softmax(s, m_s, axis=(-2, -1))
        o_slc = jnp.einsum('bgrqkl,bgqkld->bgrqd', p_s.astype(vb.dtype), vb)
        kw = lax.dynamic_slice_in_dim(k_win_pad, t0, Q_CHUNK + WINDOW, axis=2)
        vw = lax.dynamic_slice_in_dim(v_win_pad, t0, Q_CHUNK + WINDOW, axis=2)
        wpos = t0 - WINDOW + jnp.arange(Q_CHUNK + WINDOW, dtype=jnp.int32)
        d_w = tpos[:, None] - wpos[None, :]
        m_w = (d_w >= 0) & (d_w < WINDOW) & (wpos[None, :] >= 0)
        s = jnp.einsum('bgrqd,bgkd->bgrqk', qc, kw).astype(jnp.float32) + table[:, :, _t5_bucket(d_w)]
        p_w = _masked_softmax(s, m_w)
        o_win = jnp.einsum('bgrqk,bgkd->bgrqd', p_w.astype(vw.dtype), vw)
        o = gc[..., 0:1] * o_cmp + gc[..., 1:2] * o_slc + gc[..., 2:3] * o_win
        return o.astype(q.dtype)

    outs = lax.map(chunk, jnp.arange(T // Q_CHUNK, dtype=jnp.int32))
    return outs.transpose(1, 0, 4, 2, 3, 5).reshape(B, T, ATTN_WIDTH)


def _conformer_conv(u, b_glu, w_dw, b_dw, ln_g, ln_b):
    u = u + b_glu
    a, g = jnp.split(u, 2, axis=-1)
    h = a * jax.nn.sigmoid(g)
    h = lax.conv_general_dilated(h, w_dw[:, None, :], (1,), ((CONV_WIDTH - 1, 0),),
                                 dimension_numbers=('NWC', 'WIO', 'NWC'),
                                 feature_group_count=CONV_CH) + b_dw
    hf = h.astype(jnp.float32)
    mu = jnp.mean(hf, axis=-1, keepdims=True)
    var = jnp.mean(jnp.square(hf - mu), axis=-1, keepdims=True)
    hf = (hf - mu) * lax.rsqrt(var + EPS) * ln_g.astype(jnp.float32) + ln_b.astype(jnp.float32)
    return jax.nn.silu(hf).astype(u.dtype)


def setup_inputs(seed: int = 0) -> dict:
    key = jax.random.key(seed)
    ks = jax.random.split(key, 24)
    f32 = jnp.float32

    def nrm(k, shape, scale):
        return jax.random.normal(k, shape, f32) * scale

    L = DEPTH
    return {
        'x': nrm(ks[0], (BATCH, SEQ, D_MODEL), 1.0),
        'w_in': nrm(ks[1], (L, D_MODEL, IN_COLS), D_MODEL ** -0.5),
        'cmp_pos': nrm(ks[2], (L, 2, CMP_BLOCK, HEAD_DIM), 0.02),
        'cmp_w1': nrm(ks[3], (L, 2, CMP_BLOCK * HEAD_DIM, HEAD_DIM), (CMP_BLOCK * HEAD_DIM) ** -0.5),
        'cmp_w2': nrm(ks[4], (L, 2, HEAD_DIM, HEAD_DIM), HEAD_DIM ** -0.5),
        'rel_bias': nrm(ks[5], (NUM_BUCKETS, N_HEADS), 0.5),
        'w_attn_out': nrm(ks[6], (L, ATTN_WIDTH, D_MODEL), ATTN_WIDTH ** -0.5),
        'b_glu': nrm(ks[7], (L, 2 * CONV_CH), 0.02),
        'w_dw': nrm(ks[8], (L, CONV_WIDTH, CONV_CH), CONV_WIDTH ** -0.5),
        'b_dw': nrm(ks[9], (L, CONV_CH), 0.02),
        'conv_ln_g': 1.0 + nrm(ks[10], (L, CONV_CH), 0.02),
        'conv_ln_b': nrm(ks[11], (L, CONV_CH), 0.02),
        'w_conv_out': nrm(ks[12], (L, CONV_CH, D_MODEL), CONV_CH ** -0.5),
        'b_conv_out': nrm(ks[13], (L, D_MODEL), 0.02),
        'w_out': nrm(ks[14], (L, D_MODEL, D_MODEL), D_MODEL ** -0.5),
        'norm_mix': 1.0 + nrm(ks[15], (L, D_MODEL), 0.02),
        'norm_ffn': 1.0 + nrm(ks[16], (L, D_MODEL), 0.02),
        'w_ffn_gate': nrm(ks[17], (L, D_MODEL, D_FF), D_MODEL ** -0.5),
        'w_ffn_up': nrm(ks[18], (L, D_MODEL, D_FF), D_MODEL ** -0.5),
        'w_ffn_down': nrm(ks[19], (L, D_FF, D_MODEL), D_FF ** -0.5),
        'norm_final': 1.0 + nrm(ks[20], (D_MODEL,), 0.02),
    }


def reference(x, w_in, cmp_pos, cmp_w1, cmp_w2, rel_bias, w_attn_out, b_glu, w_dw, b_dw,
              conv_ln_g, conv_ln_b, w_conv_out, b_conv_out, w_out, norm_mix, norm_ffn,
              w_ffn_gate, w_ffn_up, w_ffn_down, norm_final):
    split_points = np.cumsum([ATTN_WIDTH, 6 * KV_WIDTH, 2 * CONV_CH, D_MODEL, D_MODEL]).tolist()
    for l in range(DEPTH):
        h = _rmsnorm(x, norm_mix[l])
        proj = h @ w_in[l]
        q, kv, conv_in, gate_a, gate_b, br_gates = jnp.split(proj, split_points, axis=-1)
        y_a = _nsa(q, kv, br_gates, cmp_pos[l], cmp_w1[l], cmp_w2[l], rel_bias) @ w_attn_out[l]
        y_b = _conformer_conv(conv_in, b_glu[l], w_dw[l], b_dw[l], conv_ln_g[l], conv_ln_b[l]) @ w_conv_out[l] + b_conv_out[l]
        merged = jax.nn.sigmoid(gate_a) * y_a + jax.nn.sigmoid(gate_b) * y_b
        x = x + merged @ w_out[l]
        h = _rmsnorm(x, norm_ffn[l])
        x = x + (jax.nn.silu(h @ w_ffn_gate[l]) * (h @ w_ffn_up[l])) @ w_ffn_down[l]
    return _rmsnorm(x, norm_final)
```

```python
import functools
import math

import jax
import jax.numpy as jnp
import numpy as np
from jax import lax
from jax.experimental import pallas as pl
from jax.experimental.pallas import tpu as pltpu

F32 = jnp.float32
BF16 = jnp.bfloat16

N_HEADS = 32
HEAD_DIM = 128
N_KV_GROUPS = 4
HEADS_PER_GROUP = N_HEADS // N_KV_GROUPS
CMP_BLOCK = 32
CMP_STRIDE = 16
SEL_BLOCK = 64
LOG2_SEL_BLOCK = 6
SEL_TOPK = 16
WINDOW = 512
CONV_WIDTH = 31
NUM_BUCKETS = 32
MAX_EXACT = NUM_BUCKETS // 2
MAX_DISTANCE = 128
EPS = 1e-6
NEG_INF = -1e30

LANES = 128
VMEM_LIMIT_BYTES = 56 * 1024 * 1024

Q_TILE = 128
FAR_CHUNK = 512
NEAR_KEYS = 2 * Q_TILE
WIN_KEYS = WINDOW + Q_TILE
CONV_HALO = 32


def _params(*sem):
    return pltpu.CompilerParams(dimension_semantics=sem, vmem_limit_bytes=VMEM_LIMIT_BYTES)


def _rmsnorm_kernel(x_ref, g_ref, o_ref):
    x = x_ref[...]
    ms = jnp.mean(x * x, axis=-1, keepdims=True)
    o_ref[...] = (x * lax.rsqrt(ms + EPS) * g_ref[...]).astype(o_ref.dtype)


def _rmsnorm(x, g, out_dtype, tm=256):
    m, d = x.shape
    return pl.pallas_call(
        _rmsnorm_kernel,
        grid=(m // tm,),
        in_specs=[pl.BlockSpec((tm, d), lambda i: (i, 0)),
                  pl.BlockSpec((1, d), lambda i: (0, 0))],
        out_specs=pl.BlockSpec((tm, d), lambda i: (i, 0)),
        out_shape=jax.ShapeDtypeStruct((m, d), out_dtype),
        compiler_params=_params("parallel"),
        name="rmsnorm",
    )(x, g.reshape(1, d).astype(F32))


def _dot(a, b):
    return jnp.dot(a, b, preferred_element_type=F32)


def _mm_scale_kernel(a_ref, b_ref, o_ref, *, scale):
    acc = _dot(a_ref[...], b_ref[...])
    if scale != 1.0:
        acc = acc * scale
    o_ref[...] = acc.astype(o_ref.dtype)


def _mm_glu_kernel(a_ref, b1_ref, b2_ref, v1_ref, v2_ref, o_ref):
    a = a_ref[...]
    u = _dot(a, b1_ref[...]) + v1_ref[...]
    g = _dot(a, b2_ref[...]) + v2_ref[...]
    o_ref[...] = (u * jax.nn.sigmoid(g)).astype(o_ref.dtype)


def _mm_swiglu_kernel(a_ref, b1_ref, b2_ref, o_ref):
    a = a_ref[...]
    g = _dot(a, b1_ref[...])
    u = _dot(a, b2_ref[...])
    o_ref[...] = (jax.nn.silu(g) * u).astype(o_ref.dtype)


def _mm_merge_kernel(a1_ref, a2_ref, b1_ref, b2_ref, ga_ref, gb_ref, v_ref, o_ref):
    ya = _dot(a1_ref[...], b1_ref[...])
    yb = _dot(a2_ref[...], b2_ref[...]) + v_ref[...]
    o_ref[...] = (jax.nn.sigmoid(ga_ref[...]) * ya + jax.nn.sigmoid(gb_ref[...]) * yb).astype(o_ref.dtype)


def _mm_residual_kernel(a_ref, b_ref, r_ref, o_ref):
    o_ref[...] = r_ref[...] + _dot(a_ref[...], b_ref[...])


def _mm_call(body, a_ops, b_ops, mn_ops, vec_ops, out_dtype, tm, tn, name):
    m = a_ops[0].shape[0]
    n = b_ops[0].shape[1]
    tm = min(tm, m)
    tn = min(tn, n)
    assert m % tm == 0 and n % tn == 0, (m, n, tm, tn)
    in_specs = []
    for a in a_ops:
        in_specs.append(pl.BlockSpec((tm, a.shape[1]), lambda i, j: (i, 0)))
    for b in b_ops:
        in_specs.append(pl.BlockSpec((b.shape[0], tn), lambda i, j: (0, j)))
    for _, col0 in mn_ops:
        assert col0 % tn == 0
        in_specs.append(pl.BlockSpec((tm, tn), lambda i, j, off=col0 // tn: (i, j + off)))
    mn_ops = [a for a, _ in mn_ops]
    for _ in vec_ops:
        in_specs.append(pl.BlockSpec((1, tn), lambda i, j: (0, j)))
    return pl.pallas_call(
        body,
        grid=(m // tm, n // tn),
        in_specs=in_specs,
        out_specs=pl.BlockSpec((tm, tn), lambda i, j: (i, j)),
        out_shape=jax.ShapeDtypeStruct((m, n), out_dtype),
        compiler_params=_params("parallel", "arbitrary"),
        name=name,
    )(*a_ops, *b_ops, *mn_ops, *vec_ops)


def _gelu_tanh(x):
    c = math.sqrt(2.0 / math.pi)
    return x * (0.5 * (1.0 + jnp.tanh(c * (x + 0.044715 * (x * x * x)))))


def _compress_kernel(x_ref, pos_ref, w1_ref, w2_ref, o_ref):
    x = x_ref[0, 0]
    p1 = _dot((x + pos_ref[0, 0]).astype(BF16), w1_ref[0, 0])
    p2 = _dot((x + pos_ref[0, 1]).astype(BF16), w1_ref[0, 1])
    ns = x.shape[0]
    pre = p1 + pltpu.roll(p2, ns - 1, 0)
    h = _gelu_tanh(pre).astype(BF16)
    o_ref[0, 0] = _dot(h, w2_ref[0]).astype(o_ref.dtype)


def _compress(xkv, pos, w1, w2):
    _, bg, ns, kw = xkv.shape
    dh = w2.shape[-1]
    return pl.pallas_call(
        _compress_kernel,
        grid=(2, bg),
        in_specs=[pl.BlockSpec((1, 1, ns, kw), lambda k, i: (k, i, 0, 0)),
                  pl.BlockSpec((1, 2, 1, kw), lambda k, i: (k, 0, 0, 0)),
                  pl.BlockSpec((1, 2, kw, dh), lambda k, i: (k, 0, 0, 0)),
                  pl.BlockSpec((1, dh, dh), lambda k, i: (k, 0, 0))],
        out_specs=pl.BlockSpec((1, 1, ns, dh), lambda k, i: (k, i, 0, 0)),
        out_shape=jax.ShapeDtypeStruct((2, bg, ns, dh), BF16),
        compiler_params=_params("parallel", "parallel"),
        name="nsa_compress",
    )(xkv, pos, w1, w2)


def _nt_dot(a, b):
    return lax.dot_general(a, b, (((1,), (1,)), ((), ())), preferred_element_type=F32)


def _nsa_kernel(q_ref, kslc_ref, vslc_ref, kwin_ref, vwin_ref, kcmp_ref, vcmp_ref, gates_ref,
                bnear_ref, bwin_ref, bcmp_ref, emat_ref, selmap_ref, o_ref,
                m_ref, l_ref, acc_ref):
    i = pl.program_id(2)
    tq = Q_TILE
    nh = HEADS_PER_GROUP
    rows = nh * tq
    n_cmp = kcmp_ref.shape[2]

    q = q_ref[0]
    qs = jnp.concatenate([q[:, r * HEAD_DIM:(r + 1) * HEAD_DIM] for r in range(nh)], axis=0)

    def add_shared(s, shared):
        n = s.shape[-1]
        return (s.reshape(nh, tq, n) + shared[None]).reshape(rows, n)

    t_col = lax.broadcasted_iota(jnp.int32, (tq, LANES), 0)
    lane = lax.broadcasted_iota(jnp.int32, (tq, LANES), 1)

    nrow = lax.broadcasted_iota(jnp.int32, (n_cmp, LANES), 0)
    ncol = lax.broadcasted_iota(jnp.int32, (n_cmp, LANES), 1)
    place = jnp.where((nrow == (ncol & 15) + (8 * i - 8)) & (ncol < 48), 1.0, 0.0).astype(BF16)
    q_aug = jnp.concatenate([qs, bcmp_ref[...].reshape(rows, LANES)], axis=1)
    k_aug = jnp.concatenate([kcmp_ref[0, 0], place], axis=1)
    s = _nt_dot(q_aug, k_aug)
    tc = lax.broadcasted_iota(jnp.int32, (tq, n_cmp), 0) + i * tq
    nc = lax.broadcasted_iota(jnp.int32, (tq, n_cmp), 1)
    ok_c = (tc - (nc * CMP_STRIDE + CMP_BLOCK - 1)) >= 0
    s = add_shared(s, jnp.where(ok_c, 0.0, NEG_INF))
    m = jnp.max(s, axis=-1, keepdims=True)
    e = jnp.exp(s - m)
    p = e / jnp.sum(e, axis=-1, keepdims=True)
    p3 = jnp.where(ok_c[None], p.reshape(nh, tq, n_cmp), 0.0)
    o_cmp = _dot(p3.reshape(rows, n_cmp).astype(BF16), vcmp_ref[0, 0])

    psum = jnp.sum(p3, axis=0)
    p_hi = psum.astype(BF16)
    p_lo = (psum - p_hi.astype(F32)).astype(BF16)
    imp = _dot(p_hi, selmap_ref[...]) + _dot(p_lo, selmap_ref[...])
    cur = (t_col + i * tq) >> LOG2_SEL_BLOCK
    valid = lane <= cur
    forced = (lane == 0) | (lane == cur) | (lane == cur - 1)
    big = 1e30
    imp = jnp.where(valid, jnp.where(forced, big, imp), -big)
    cnt = jnp.zeros((tq, LANES), F32)
    n_blocks = emat_ref.shape[1] // SEL_BLOCK
    assert n_blocks <= LANES
    for j in range(n_blocks):
        col = imp[:, j:j + 1]
        cnt = cnt + jnp.where(col > imp, 1.0, 0.0) + jnp.where((col == imp) & (lane > j), 1.0, 0.0)
    sel = jnp.where(valid & (cnt < SEL_TOPK), 1.0, 0.0)

    m_ref[...] = jnp.full(m_ref.shape, NEG_INF, F32)
    l_ref[...] = jnp.zeros(l_ref.shape, F32)
    acc_ref[...] = jnp.zeros(acc_ref.shape, F32)

    def online_update(s, v):
        m_prev = m_ref[...]
        m_next = jnp.maximum(m_prev, jnp.max(s, axis=-1, keepdims=True))
        alpha = jnp.exp(m_prev - m_next)
        p = jnp.exp(s - pltpu.repeat(m_next, s.shape[-1] // LANES, axis=1))
        l_ref[...] = alpha * l_ref[...] + jnp.sum(p, axis=-1, keepdims=True)
        acc_ref[...] = alpha * acc_ref[...] + _dot(p.astype(BF16), v)
        m_ref[...] = m_next

    sel_far = jnp.where(lane <= 2 * i - 3, sel, 0.0).astype(BF16)

    def far_body(c, carry):
        r0 = pl.multiple_of(c * FAR_CHUNK, FAR_CHUNK)
        k = kslc_ref[0, pl.ds(r0 + Q_TILE, FAR_CHUNK), :]
        v = vslc_ref[0, pl.ds(r0 + Q_TILE, FAR_CHUNK), :]
        hit = _dot(sel_far, emat_ref[:, pl.ds(r0, FAR_CHUNK)])
        s = add_shared(_nt_dot(qs, k), (hit - 1.0) * (-NEG_INF))
        online_update(s, v)
        return carry

    lax.fori_loop(0, (i + 2) // (FAR_CHUNK // Q_TILE), far_body, 0)

    r0 = pl.multiple_of(i * tq, tq)
    k = kslc_ref[0, pl.ds(r0, NEAR_KEYS), :]
    v = vslc_ref[0, pl.ds(r0, NEAR_KEYS), :]
    erow = lax.broadcasted_iota(jnp.int32, (LANES, NEAR_KEYS), 0)
    ecol = lax.broadcasted_iota(jnp.int32, (LANES, NEAR_KEYS), 1)
    e_near = jnp.where(erow == 2 * i - 2 + (ecol >> LOG2_SEL_BLOCK), 1.0, 0.0).astype(BF16)
    hit = _dot(sel.astype(BF16), e_near)
    s = add_shared(_nt_dot(qs, k), (hit - 1.0) * (-NEG_INF)) + bnear_ref[...].reshape(rows, NEAR_KEYS)
    online_update(s, v)
    o_slc = acc_ref[...] / l_ref[...]

    k = kwin_ref[0, pl.ds(r0, WIN_KEYS), :]
    v = vwin_ref[0, pl.ds(r0, WIN_KEYS), :]
    wcol = lax.broadcasted_iota(jnp.int32, (tq, WIN_KEYS), 1)
    in_seq = wcol >= WINDOW - i * tq
    s = _nt_dot(qs, k) + bwin_ref[...].reshape(rows, WIN_KEYS)
    s = add_shared(s, jnp.where(in_seq, 0.0, NEG_INF))
    m = jnp.max(s, axis=-1, keepdims=True)
    e = jnp.exp(s - m)
    o_win = _dot(e.astype(BF16), v) / jnp.sum(e, axis=-1, keepdims=True)

    gates = jax.nn.sigmoid(gates_ref[0, 0])
    for r in range(nh):
        sl = slice(r * tq, (r + 1) * tq)
        o = (gates[:, 3 * r:3 * r + 1] * o_cmp[sl]
             + gates[:, 3 * r + 1:3 * r + 2] * o_slc[sl]
             + gates[:, 3 * r + 2:3 * r + 3] * o_win[sl])
        o_ref[0, :, r * HEAD_DIM:(r + 1) * HEAD_DIM] = o.astype(o_ref.dtype)


def _nsa_attention(q, kslc, vslc, kwin, vwin, kvcmp, gates, bnear, bwin, bcmp, emat, selmap):
    b, t, _ = q.shape
    g = N_KV_GROUPS
    nh = HEADS_PER_GROUP
    dh = HEAD_DIM
    tq = Q_TILE
    n_cmp = kvcmp.shape[2]
    tp_slc = kslc.shape[1]
    tp_win = kwin.shape[1]
    rows = nh * tq
    kv_spec = lambda tp: pl.BlockSpec((1, tp, dh), lambda bi, gi, ti: (bi, 0, gi))
    return pl.pallas_call(
        _nsa_kernel,
        grid=(b, g, t // tq),
        in_specs=[
            pl.BlockSpec((1, tq, nh * dh), lambda bi, gi, ti: (bi, ti, gi)),
            kv_spec(tp_slc), kv_spec(tp_slc), kv_spec(tp_win), kv_spec(tp_win),
            pl.BlockSpec((1, 1, n_cmp, dh), lambda bi, gi, ti: (0, bi * N_KV_GROUPS + gi, 0, 0)),
            pl.BlockSpec((1, 1, n_cmp, dh), lambda bi, gi, ti: (1, bi * N_KV_GROUPS + gi, 0, 0)),
            pl.BlockSpec((1, 1, tq, LANES), lambda bi, gi, ti: (bi, gi, ti, 0)),
            pl.BlockSpec((nh, tq, NEAR_KEYS), lambda bi, gi, ti: (gi, 0, 0)),
            pl.BlockSpec((nh, tq, WIN_KEYS), lambda bi, gi, ti: (gi, 0, 0)),
            pl.BlockSpec((nh, tq, LANES), lambda bi, gi, ti: (gi, 0, 0)),
            pl.BlockSpec(emat.shape, lambda bi, gi, ti: (0, 0)),
            pl.BlockSpec(selmap.shape, lambda bi, gi, ti: (0, 0)),
        ],
        out_specs=pl.BlockSpec((1, tq, nh * dh), lambda bi, gi, ti: (bi, ti, gi)),
        out_shape=jax.ShapeDtypeStruct((b, t, g * nh * dh), BF16),
        scratch_shapes=[pltpu.VMEM((rows, LANES), F32),
                        pltpu.VMEM((rows, LANES), F32),
                        pltpu.VMEM((rows, dh), F32)],
        compiler_params=_params("parallel", "parallel", "arbitrary"),
        name="nsa_attention",
    )(q, kslc, vslc, kwin, vwin, kvcmp, kvcmp, gates, bnear, bwin, bcmp, emat, selmap)


def _t5_bucket_table(max_dist):
    d = np.arange(max_dist)
    nf = np.maximum(d, MAX_EXACT).astype(np.float32)
    large = MAX_EXACT + (np.log(nf / np.float32(MAX_EXACT)) / np.float32(math.log(MAX_DISTANCE / MAX_EXACT))
                         * np.float32(NUM_BUCKETS - MAX_EXACT)).astype(np.int32)
    large = np.minimum(large, NUM_BUCKETS - 1)
    return np.where(d < MAX_EXACT, d, large).astype(np.int32)


def _nsa_constants(rel_bias, seq):
    tq = Q_TILE
    max_d = WIN_KEYS + tq
    bucket = _t5_bucket_table(max_d)
    far_cmp = tq - (CMP_BLOCK - 1) + CMP_STRIDE
    assert np.all(bucket[min(far_cmp, tq + 1):] == NUM_BUCKETS - 1)
    table = rel_bias.astype(F32).T
    delta = table[:, bucket] - table[:, NUM_BUCKETS - 1:]

    def toeplitz(d, ok):
        vals = delta[:, np.clip(d, 0, max_d - 1)]
        return jnp.where(jnp.asarray(ok)[None], vals, NEG_INF)

    t = np.arange(tq)[:, None]
    c = np.arange(NEAR_KEYS)[None, :]
    d_near = t + tq - c
    bnear = toeplitz(d_near, d_near >= 0)
    c = np.arange(WIN_KEYS)[None, :]
    d_win = t + WINDOW - c
    bwin = toeplitz(d_win, (d_win >= 0) & (d_win < WINDOW))
    npr = np.arange(16)[None, :]
    d_cmp = t - CMP_STRIDE * npr + (8 * CMP_STRIDE - (CMP_BLOCK - 1))
    bc = jnp.where(jnp.asarray(d_cmp >= 0)[None], delta[:, np.clip(d_cmp, 0, max_d - 1)], 0.0)
    hi = bc.astype(BF16)
    mid = (bc - hi.astype(F32)).astype(BF16)
    lo = (bc - hi.astype(F32) - mid.astype(F32)).astype(BF16)
    bcmp = jnp.concatenate([hi, mid, lo, jnp.zeros((N_HEADS, tq, LANES - 48), BF16)], axis=-1)

    nsb = seq // SEL_BLOCK
    emat = np.zeros((LANES, seq), np.float32)
    emat[np.arange(seq) // SEL_BLOCK, np.arange(seq)] = 1.0
    ns = seq // CMP_STRIDE
    c_start = np.arange(ns) * CMP_STRIDE
    s_start = np.arange(nsb) * SEL_BLOCK
    ov = np.minimum(c_start[:, None] + CMP_BLOCK, s_start[None, :] + SEL_BLOCK) - np.maximum(c_start[:, None], s_start[None, :])
    ov = np.clip(ov, 0, None) // CMP_STRIDE
    selmap = np.zeros((ns, LANES), np.float32)
    selmap[:ns - 1, :nsb] = ov[:ns - 1]
    return bnear, bwin, bcmp, jnp.asarray(emat, BF16), jnp.asarray(selmap, BF16)


def _conv_kernel(cur_ref, prev_ref, w_ref, b_ref, g_ref, beta_ref, o_ref, buf_ref):
    tt = cur_ref.shape[1]
    first = pl.program_id(1) == 0
    buf_ref[0:CONV_HALO, :] = jnp.where(first, 0.0, prev_ref[0])
    buf_ref[CONV_HALO:, :] = cur_ref[0]
    acc = jnp.zeros(cur_ref.shape[1:], F32) + b_ref[...]
    base = CONV_HALO - (CONV_WIDTH - 1)
    for k in range(CONV_WIDTH):
        acc = acc + w_ref[k:k + 1, :] * buf_ref[base + k:base + k + tt, :]
    mu = jnp.mean(acc, axis=-1, keepdims=True)
    xc = acc - mu
    var = jnp.mean(xc * xc, axis=-1, keepdims=True)
    y = xc * lax.rsqrt(var + EPS) * g_ref[...] + beta_ref[...]
    o_ref[0] = jax.nn.silu(y).astype(o_ref.dtype)


def _conformer_conv(h, w_dw, b_dw, ln_g, ln_b, tt=128):
    b, t, c = h.shape
    hb = tt // CONV_HALO
    wpad = jnp.pad(w_dw.astype(F32), ((0, 32 - CONV_WIDTH), (0, 0)))
    vec = lambda a: a.reshape(1, c).astype(F32)
    return pl.pallas_call(
        _conv_kernel,
        grid=(b, t // tt),
        in_specs=[pl.BlockSpec((1, tt, c), lambda bi, ti: (bi, ti, 0)),
                  pl.BlockSpec((1, CONV_HALO, c), lambda bi, ti: (bi, jnp.maximum(ti * hb - 1, 0), 0)),
                  pl.BlockSpec((32, c), lambda bi, ti: (0, 0)),
                  pl.BlockSpec((1, c), lambda bi, ti: (0, 0)),
                  pl.BlockSpec((1, c), lambda bi, ti: (0, 0)),
                  pl.BlockSpec((1, c), lambda bi, ti: (0, 0))],
        out_specs=pl.BlockSpec((1, tt, c), lambda bi, ti: (bi, ti, 0)),
        out_shape=jax.ShapeDtypeStruct((b, t, c), BF16),
        scratch_shapes=[pltpu.VMEM((CONV_HALO + tt, c), F32)],
        compiler_params=_params("parallel", "arbitrary"),
        name="conformer_conv",
    )(h, h, wpad, vec(b_dw), vec(ln_g), vec(ln_b))


def _nsa(h, w_in_l, cmp_pos, cmp_w1, cmp_w2, consts, batch, seq):
    dh = HEAD_DIM
    g = N_KV_GROUPS
    attn_w = N_HEADS * dh
    kv_w = g * dh
    m = h.shape[0]
    c0 = attn_w
    wb = lambda a, b: w_in_l[:, a:b].astype(BF16)
    q = _mm_call(functools.partial(_mm_scale_kernel, scale=dh ** -0.5), [h], [wb(0, attn_w)], [], [],
                 BF16, 1024, 512, "proj_q")
    kvc = _mm_call(functools.partial(_mm_scale_kernel, scale=1.0), [h], [wb(c0, c0 + 2 * kv_w)], [], [],
                   F32, 1024, 512, "proj_kv_cmp")
    kvr = _mm_call(functools.partial(_mm_scale_kernel, scale=1.0), [h], [wb(c0 + 2 * kv_w, c0 + 6 * kv_w)], [], [],
                   BF16, 1024, 512, "proj_kv")
    n_gates = 3 * N_HEADS
    w_br = jnp.pad(w_in_l[:, -n_gates:], ((0, 0), (0, LANES - n_gates))).astype(BF16)
    br = _mm_call(functools.partial(_mm_scale_kernel, scale=1.0), [h], [w_br], [], [],
                  F32, 1024, LANES, "proj_branch_gates")

    ns = seq // CMP_STRIDE
    xkv = kvc.reshape(batch, ns, CMP_STRIDE, 2, g, dh).transpose(3, 0, 4, 1, 2, 5).reshape(2, batch * g, ns, CMP_STRIDE * dh)
    pos = cmp_pos.astype(F32).reshape(2, 2, 1, CMP_STRIDE * dh)
    w1 = cmp_w1.astype(BF16).reshape(2, 2, CMP_STRIDE * dh, dh)
    kvcmp = _compress(xkv, pos, w1, cmp_w2.astype(BF16))

    kvr = kvr.reshape(batch, seq, 4 * kv_w)
    padt = lambda a, n: jnp.pad(a, ((0, 0), (n, 0), (0, 0)))
    kslc = padt(kvr[:, :, 0:kv_w], Q_TILE)
    vslc = padt(kvr[:, :, kv_w:2 * kv_w], Q_TILE)
    kwin = padt(kvr[:, :, 2 * kv_w:3 * kv_w], WINDOW)
    vwin = padt(kvr[:, :, 3 * kv_w:4 * kv_w], WINDOW)
    per_g = 3 * HEADS_PER_GROUP
    gates = br[:, :n_gates].reshape(batch, seq, g, per_g).transpose(0, 2, 1, 3)
    gates = jnp.pad(gates, ((0, 0), (0, 0), (0, 0), (0, LANES - per_g)))
    o = _nsa_attention(q.reshape(batch, seq, attn_w), kslc, vslc, kwin, vwin, kvcmp, gates, *consts)
    return o.reshape(m, attn_w)


def kernel(x, w_in, cmp_pos, cmp_w1, cmp_w2, rel_bias, w_attn_out, b_glu, w_dw, b_dw, conv_ln_g, conv_ln_b, w_conv_out, b_conv_out, w_out, norm_mix, norm_ffn, w_ffn_gate, w_ffn_up, w_ffn_down, norm_final):
    batch, seq, d = x.shape
    depth = w_in.shape[0]
    m = batch * seq
    attn_w = N_HEADS * HEAD_DIM
    kv_w = N_KV_GROUPS * HEAD_DIM
    conv_ch = w_dw.shape[-1]
    c_conv = attn_w + 6 * kv_w
    c_gate = c_conv + 2 * conv_ch
    consts = _nsa_constants(rel_bias, seq)
    xf = x.reshape(m, d).astype(F32)
    row = lambda a: a.reshape(1, -1).astype(F32)
    for l in range(depth):
        w_in_l = w_in[l]
        h = _rmsnorm(xf, norm_mix[l], BF16)
        attn = _nsa(h, w_in_l, cmp_pos[l], cmp_w1[l], cmp_w2[l], consts, batch, seq)
        glu = _mm_call(_mm_glu_kernel, [h],
                       [w_in_l[:, c_conv:c_conv + conv_ch].astype(BF16),
                        w_in_l[:, c_conv + conv_ch:c_gate].astype(BF16)],
                       [], [row(b_glu[l, :conv_ch]), row(b_glu[l, conv_ch:])], F32, 1024, 512, "proj_conv_glu")
        conv = _conformer_conv(glu.reshape(batch, seq, conv_ch), w_dw[l], b_dw[l], conv_ln_g[l], conv_ln_b[l])
        gate_ab = _mm_call(functools.partial(_mm_scale_kernel, scale=1.0), [h],
                           [w_in_l[:, c_gate:c_gate + 2 * d].astype(BF16)], [], [], F32, 1024, 512, "proj_merge_gates")
        merged = _mm_call(_mm_merge_kernel, [attn, conv.reshape(m, conv_ch)],
                          [w_attn_out[l].astype(BF16), w_conv_out[l].astype(BF16)],
                          [(gate_ab, 0), (gate_ab, d)], [row(b_conv_out[l])], BF16, 512, 512, "merge")
        xf = _mm_call(_mm_residual_kernel, [merged], [w_out[l].astype(BF16)], [(xf, 0)], [], F32, 1024, 512, "mix_out")
        h = _rmsnorm(xf, norm_ffn[l], BF16)
        gu = _mm_call(_mm_swiglu_kernel, [h], [w_ffn_gate[l].astype(BF16), w_ffn_up[l].astype(BF16)],
                      [], [], BF16, 1024, 256, "ffn_swiglu")
        xf = _mm_call(_mm_residual_kernel, [gu], [w_ffn_down[l].astype(BF16)], [(xf, 0)], [], F32, 512, 256, "ffn_down")
    out = _rmsnorm(xf, norm_final, F32)
    return out.reshape(batch, seq, d).astype(x.dtype)
```

```python
import functools
import math

import jax
import jax.numpy as jnp
import numpy as np
from jax import lax
from jax.experimental import pallas as pl
from jax.experimental.pallas import tpu as pltpu

F32 = jnp.float32
BF16 = jnp.bfloat16

N_HEADS = 32
HEAD_DIM = 128
N_KV_GROUPS = 4
HEADS_PER_GROUP = N_HEADS // N_KV_GROUPS
CMP_BLOCK = 32
CMP_STRIDE = 16
SEL_BLOCK = 64
LOG2_SEL_BLOCK = 6
SEL_TOPK = 16
WINDOW = 512
CONV_WIDTH = 31
NUM_BUCKETS = 32
MAX_EXACT = NUM_BUCKETS // 2
MAX_DISTANCE = 128
EPS = 1e-6
NEG_INF = -1e30

LANES = 128
SUBLANES = 8
VMEM_LIMIT_BYTES = 56 * 1024 * 1024
LOG2_E = math.log2(math.e)

Q_TILE = 128
FAR_CHUNK = 512
NEAR_KEYS = 2 * Q_TILE
WIN_KEYS = WINDOW + Q_TILE
CMP_HEADS = 8
FAR_HEADS = 1
NEAR_HEADS = 8
WIN_HEADS = 4
CONV_HALO = 32
CONV_LANE_CHUNK = 512
CONV_ROW_BLOCK = 32


def _params(*sem):
    return pltpu.CompilerParams(dimension_semantics=sem, vmem_limit_bytes=VMEM_LIMIT_BYTES)


def _rmsnorm_kernel(x_ref, g_ref, o_ref):
    x = x_ref[...]
    ms = jnp.mean(x * x, axis=-1, keepdims=True)
    o_ref[...] = (x * lax.rsqrt(ms + EPS) * g_ref[...]).astype(o_ref.dtype)


def _rmsnorm(x, g, out_dtype, tm=256):
    m, d = x.shape
    return pl.pallas_call(
        _rmsnorm_kernel,
        grid=(m // tm,),
        in_specs=[pl.BlockSpec((tm, d), lambda i: (i, 0)),
                  pl.BlockSpec((1, d), lambda i: (0, 0))],
        out_specs=pl.BlockSpec((tm, d), lambda i: (i, 0)),
        out_shape=jax.ShapeDtypeStruct((m, d), out_dtype),
        compiler_params=_params("parallel"),
        name="rmsnorm",
    )(x, g.reshape(1, d).astype(F32))


def _dot(a, b):
    return jnp.dot(a, b, preferred_element_type=F32)


def _mm_scale_kernel(a_ref, b_ref, o_ref, *, scale):
    acc = _dot(a_ref[...], b_ref[...])
    if scale != 1.0:
        acc = acc * scale
    o_ref[...] = acc.astype(o_ref.dtype)


def _mm_glu_kernel(a_ref, b1_ref, b2_ref, v1_ref, v2_ref, o_ref):
    a = a_ref[...]
    u = _dot(a, b1_ref[...]) + v1_ref[...]
    g = _dot(a, b2_ref[...]) + v2_ref[...]
    o_ref[...] = (u * jax.nn.sigmoid(g)).astype(o_ref.dtype)


def _mm_swiglu_kernel(a_ref, b1_ref, b2_ref, o_ref):
    a = a_ref[...]
    g = _dot(a, b1_ref[...])
    u = _dot(a, b2_ref[...])
    o_ref[...] = (jax.nn.silu(g) * u).astype(o_ref.dtype)


def _mm_merge_kernel(a1_ref, a2_ref, b1_ref, b2_ref, ga_ref, gb_ref, v_ref, o_ref):
    ya = _dot(a1_ref[...], b1_ref[...])
    yb = _dot(a2_ref[...], b2_ref[...]) + v_ref[...]
    o_ref[...] = (jax.nn.sigmoid(ga_ref[...]) * ya + jax.nn.sigmoid(gb_ref[...]) * yb).astype(o_ref.dtype)


def _mm_residual_kernel(a_ref, b_ref, r_ref, o_ref):
    o_ref[...] = r_ref[...] + _dot(a_ref[...], b_ref[...])


def _mm_call(body, a_ops, b_ops, mn_ops, vec_ops, out_dtype, tm, tn, name):
    m = a_ops[0].shape[0]
    n = b_ops[0].shape[1]
    tm = min(tm, m)
    tn = min(tn, n)
    assert m % tm == 0 and n % tn == 0, (m, n, tm, tn)
    in_specs = []
    for a in a_ops:
        in_specs.append(pl.BlockSpec((tm, a.shape[1]), lambda i, j: (i, 0)))
    for b in b_ops:
        in_specs.append(pl.BlockSpec((b.shape[0], tn), lambda i, j: (0, j)))
    for _, col0 in mn_ops:
        assert col0 % tn == 0
        in_specs.append(pl.BlockSpec((tm, tn), lambda i, j, off=col0 // tn: (i, j + off)))
    mn_ops = [a for a, _ in mn_ops]
    for _ in vec_ops:
        in_specs.append(pl.BlockSpec((1, tn), lambda i, j: (0, j)))
    return pl.pallas_call(
        body,
        grid=(m // tm, n // tn),
        in_specs=in_specs,
        out_specs=pl.BlockSpec((tm, tn), lambda i, j: (i, j)),
        out_shape=jax.ShapeDtypeStruct((m, n), out_dtype),
        compiler_params=_params("parallel", "arbitrary"),
        name=name,
    )(*a_ops, *b_ops, *mn_ops, *vec_ops)


def _gelu_tanh(x):
    c = math.sqrt(2.0 / math.pi)
    return x * (0.5 * (1.0 + jnp.tanh(c * (x + 0.044715 * (x * x * x)))))


def _compress_kernel(x_ref, pos_ref, w1_ref, w2_ref, o_ref):
    x = x_ref[0, 0]
    p1 = _dot((x + pos_ref[0, 0]).astype(BF16), w1_ref[0, 0])
    p2 = _dot((x + pos_ref[0, 1]).astype(BF16), w1_ref[0, 1])
    ns = x.shape[0]
    pre = p1 + pltpu.roll(p2, ns - 1, 0)
    h = _gelu_tanh(pre).astype(BF16)
    o_ref[0, 0] = _dot(h, w2_ref[0]).astype(o_ref.dtype)


def _compress(xkv, pos, w1, w2):
    _, bg, ns, kw = xkv.shape
    dh = w2.shape[-1]
    return pl.pallas_call(
        _compress_kernel,
        grid=(2, bg),
        in_specs=[pl.BlockSpec((1, 1, ns, kw), lambda k, i: (k, i, 0, 0)),
                  pl.BlockSpec((1, 2, 1, kw), lambda k, i: (k, 0, 0, 0)),
                  pl.BlockSpec((1, 2, kw, dh), lambda k, i: (k, 0, 0, 0)),
                  pl.BlockSpec((1, dh, dh), lambda k, i: (k, 0, 0))],
        out_specs=pl.BlockSpec((1, 1, ns, dh), lambda k, i: (k, i, 0, 0)),
        out_shape=jax.ShapeDtypeStruct((2, bg, ns, dh), BF16),
        compiler_params=_params("parallel", "parallel"),
        name="nsa_compress",
    )(xkv, pos, w1, w2)


def _nt_dot(a, b):
    return lax.dot_general(a, b, (((1,), (1,)), ((), ())), preferred_element_type=F32)


def _nsa_kernel(q_ref, kslc_ref, vslc_ref, kwin_ref, vwin_ref, kcmp_ref, vcmp_ref, gates_ref,
                bnear_ref, bcmp_ref, emat_ref, selmap_t_ref, o_ref,
                m_ref, acc_ref):
    i = pl.program_id(2)
    tq = Q_TILE
    nh = HEADS_PER_GROUP
    dh = HEAD_DIM
    rows = nh * tq
    n_cmp = kcmp_ref.shape[2]

    def q_block(r0, hb):
        return jnp.concatenate([q_ref[0, :, r * dh:(r + 1) * dh] for r in range(r0, r0 + hb)], axis=0)

    def stack_heads(ref, r0, hb):
        return ref[r0:r0 + hb].reshape(hb * tq, ref.shape[-1])

    def block_rows(r0, hb):
        return slice(r0 * tq, (r0 + hb) * tq)

    def add_shared(s, shared):
        hb, n = s.shape[0] // tq, s.shape[1]
        return (s.reshape(hb, tq, n) + shared[None]).reshape(hb * tq, n)

    lane = lax.broadcasted_iota(jnp.int32, (tq, LANES), 1)

    nrow = lax.broadcasted_iota(jnp.int32, (n_cmp, LANES), 0)
    ncol = lax.broadcasted_iota(jnp.int32, (n_cmp, LANES), 1)
    place = jnp.where((nrow == (ncol & 15) + (8 * i - 8)) & (ncol < 48), 1.0, 0.0).astype(BF16)
    k_aug = jnp.concatenate([kcmp_ref[0, 0], place], axis=1)
    tc = lax.broadcasted_iota(jnp.int32, (tq, n_cmp), 0) + i * tq
    nc = lax.broadcasted_iota(jnp.int32, (tq, n_cmp), 1)
    ok_c = (tc - (nc * CMP_STRIDE + CMP_BLOCK - 1)) >= 0
    mask_c = jnp.where(ok_c, 0.0, NEG_INF)
    v_cmp = vcmp_ref[0, 0]
    o_cmp = []
    psum = jnp.zeros((tq, n_cmp), F32)
    for r0 in range(0, nh, CMP_HEADS):
        q_aug = jnp.concatenate([q_block(r0, CMP_HEADS), stack_heads(bcmp_ref, r0, CMP_HEADS)], axis=1)
        s = add_shared(_nt_dot(q_aug, k_aug), mask_c)
        e = jnp.exp2(s - jnp.max(s, axis=-1, keepdims=True))
        p = e / jnp.sum(e, axis=-1, keepdims=True)
        p3 = jnp.where(ok_c[None], p.reshape(CMP_HEADS, tq, n_cmp), 0.0)
        o_cmp.append(_dot(p3.reshape(CMP_HEADS * tq, n_cmp).astype(BF16), v_cmp))
        psum = psum + jnp.sum(p3, axis=0)
    o_cmp = jnp.concatenate(o_cmp, axis=0)

    p_hi = psum.astype(BF16)
    p_lo = (psum - p_hi.astype(F32)).astype(BF16)
    imp_t = _nt_dot(selmap_t_ref[...], p_hi) + _nt_dot(selmap_t_ref[...], p_lo)
    n_blocks = emat_ref.shape[1] // SEL_BLOCK
    assert n_blocks <= LANES and n_blocks % SUBLANES == 0
    blk = lax.broadcasted_iota(jnp.int32, (n_blocks, tq), 0)
    cur_t = (lax.broadcasted_iota(jnp.int32, (n_blocks, tq), 1) + i * tq) >> LOG2_SEL_BLOCK
    valid_t = blk <= cur_t
    forced_t = (blk == 0) | (blk == cur_t) | (blk == cur_t - 1)
    big = 1e30
    x = jnp.where(valid_t, jnp.where(forced_t, big, imp_t[:n_blocks]), -big)
    bcast = [jnp.broadcast_to(x[j:j + 1], (SUBLANES, tq)) for j in range(n_blocks)]
    cnt_groups = []
    for g0 in range(0, n_blocks, SUBLANES):
        xg = x[g0:g0 + SUBLANES]
        blk_g = blk[g0:g0 + SUBLANES]
        cnt = jnp.zeros((SUBLANES, tq), F32)
        for j in range(n_blocks):
            if j < g0:
                beats = bcast[j] >= xg
            elif j >= g0 + SUBLANES:
                beats = bcast[j] > xg
            else:
                beats = (bcast[j] > xg) | ((bcast[j] == xg) & (blk_g > j))
            cnt = cnt + jnp.where(beats, 1.0, 0.0)
        cnt_groups.append(cnt)
    cnt = jnp.concatenate(cnt_groups, axis=0)
    sel_t = jnp.where(valid_t & (cnt < SEL_TOPK), 1.0, 0.0)
    sel_t = jnp.concatenate([sel_t, jnp.zeros((LANES - n_blocks, tq), F32)], axis=0)
    sel = sel_t.T

    m_ref[...] = jnp.full(m_ref.shape, NEG_INF, F32)
    acc_ref[...] = jnp.zeros(acc_ref.shape, F32)

    def online_update(sl, s, v_ones):
        m_prev = m_ref[sl]
        m_next = jnp.maximum(m_prev, jnp.max(s, axis=-1, keepdims=True))
        alpha = jnp.exp2(m_prev - m_next)
        p = jnp.exp2(s - pltpu.repeat(m_next, s.shape[-1] // LANES, axis=1))
        acc_ref[sl] = pltpu.repeat(alpha, 2, axis=1) * acc_ref[sl] + _dot(p.astype(BF16), v_ones)
        m_ref[sl] = m_next

    sel_far = jnp.where(lane <= 2 * i - 3, sel, 0.0).astype(BF16)

    def far_body(c, carry):
        r0 = pl.multiple_of(c * FAR_CHUNK, FAR_CHUNK)
        k = kslc_ref[0, pl.ds(r0 + Q_TILE, FAR_CHUNK), :]
        v = vslc_ref[0, pl.ds(r0 + Q_TILE, FAR_CHUNK), :]
        hit = _dot(sel_far, emat_ref[:, pl.ds(r0, FAR_CHUNK)])
        mask = (hit - 1.0) * (-NEG_INF)
        for h0 in range(0, nh, FAR_HEADS):
            online_update(block_rows(h0, FAR_HEADS), add_shared(_nt_dot(q_block(h0, FAR_HEADS), k), mask), v)
        return carry

    tiles_per_chunk = FAR_CHUNK // Q_TILE
    lax.fori_loop(0, (i + tiles_per_chunk - 2) // tiles_per_chunk, far_body, 0)

    r0 = pl.multiple_of(i * tq, tq)
    k = kslc_ref[0, pl.ds(r0, NEAR_KEYS), :]
    v = vslc_ref[0, pl.ds(r0, NEAR_KEYS), :]
    erow = lax.broadcasted_iota(jnp.int32, (LANES, NEAR_KEYS), 0)
    ecol = lax.broadcasted_iota(jnp.int32, (LANES, NEAR_KEYS), 1)
    e_near = jnp.where(erow == 2 * i - 2 + (ecol >> LOG2_SEL_BLOCK), 1.0, 0.0).astype(BF16)
    hit = _dot(sel.astype(BF16), e_near)
    mask = (hit - 1.0) * (-NEG_INF)
    for h0 in range(0, nh, NEAR_HEADS):
        s = add_shared(_nt_dot(q_block(h0, NEAR_HEADS), k), mask) + stack_heads(bnear_ref, h0, NEAR_HEADS)
        online_update(block_rows(h0, NEAR_HEADS), s, v)

    k_w = kwin_ref[0, pl.ds(r0, WIN_KEYS), :]
    v_w = vwin_ref[0, pl.ds(r0, WIN_KEYS), :]
    wrow = lax.broadcasted_iota(jnp.int32, (tq, WINDOW), 0)
    wcol = lax.broadcasted_iota(jnp.int32, (tq, WINDOW), 1)
    ok_w = (wcol >= WINDOW - i * tq) & ((wcol >= tq) | (wcol > wrow))
    mask_w = jnp.where(ok_w, 0.0, NEG_INF)
    old = WIN_KEYS - NEAR_KEYS
    gates = jax.nn.sigmoid(gates_ref[0, 0])
    for h0 in range(0, nh, WIN_HEADS):
        s = _nt_dot(q_block(h0, WIN_HEADS), k_w)
        bias = stack_heads(bnear_ref, h0, WIN_HEADS)
        s = jnp.concatenate([add_shared(s[:, :old], mask_w[:, :old]),
                             add_shared(s[:, old:WINDOW], mask_w[:, old:]) + bias[:, :WINDOW - old],
                             s[:, WINDOW:] + bias[:, WINDOW - old:]], axis=1)
        pv = _dot(jnp.exp2(s - jnp.max(s, axis=-1, keepdims=True)).astype(BF16), v_w)
        o_win = pv[:, :dh] / pv[:, dh:]
        acc = acc_ref[block_rows(h0, WIN_HEADS)]
        o_slc = acc[:, :dh] / acc[:, dh:]
        for r in range(h0, h0 + WIN_HEADS):
            sl = block_rows(r - h0, 1)
            o = (gates[:, 3 * r:3 * r + 1] * o_cmp[block_rows(r, 1)]
                 + gates[:, 3 * r + 1:3 * r + 2] * o_slc[sl]
                 + gates[:, 3 * r + 2:3 * r + 3] * o_win[sl])
            o_ref[0, :, r * dh:(r + 1) * dh] = o.astype(o_ref.dtype)


def _nsa_attention(q, kslc, vslc, kwin, vwin, kvcmp, gates, bnear, bcmp, emat, selmap_t):
    b, t, _ = q.shape
    g = N_KV_GROUPS
    nh = HEADS_PER_GROUP
    dh = HEAD_DIM
    tq = Q_TILE
    n_cmp = kvcmp.shape[2]
    tp_slc = kslc.shape[1]
    tp_win = kwin.shape[1]
    rows = nh * tq
    kv_spec = lambda tp, w: pl.BlockSpec((1, tp, w), lambda bi, gi, ti: (bi, 0, gi))
    return pl.pallas_call(
        _nsa_kernel,
        grid=(b, g, t // tq),
        in_specs=[
            pl.BlockSpec((1, tq, nh * dh), lambda bi, gi, ti: (bi, ti, gi)),
            kv_spec(tp_slc, dh), kv_spec(tp_slc, 2 * dh), kv_spec(tp_win, dh), kv_spec(tp_win, 2 * dh),
            pl.BlockSpec((1, 1, n_cmp, dh), lambda bi, gi, ti: (0, bi * N_KV_GROUPS + gi, 0, 0)),
            pl.BlockSpec((1, 1, n_cmp, dh), lambda bi, gi, ti: (1, bi * N_KV_GROUPS + gi, 0, 0)),
            pl.BlockSpec((1, 1, tq, LANES), lambda bi, gi, ti: (bi, gi, ti, 0)),
            pl.BlockSpec((nh, tq, NEAR_KEYS), lambda bi, gi, ti: (gi, 0, 0)),
            pl.BlockSpec((nh, tq, LANES), lambda bi, gi, ti: (gi, 0, 0)),
            pl.BlockSpec(emat.shape, lambda bi, gi, ti: (0, 0)),
            pl.BlockSpec(selmap_t.shape, lambda bi, gi, ti: (0, 0)),
        ],
        out_specs=pl.BlockSpec((1, tq, nh * dh), lambda bi, gi, ti: (bi, ti, gi)),
        out_shape=jax.ShapeDtypeStruct((b, t, g * nh * dh), BF16),
        scratch_shapes=[pltpu.VMEM((rows, LANES), F32),
                        pltpu.VMEM((rows, 2 * dh), F32)],
        compiler_params=_params("parallel", "parallel", "arbitrary"),
        name="nsa_attention",
    )(q, kslc, vslc, kwin, vwin, kvcmp, kvcmp, gates, bnear, bcmp, emat, selmap_t)


def _t5_bucket_table(max_dist):
    d = np.arange(max_dist)
    nf = np.maximum(d, MAX_EXACT).astype(np.float32)
    large = MAX_EXACT + (np.log(nf / np.float32(MAX_EXACT)) / np.float32(math.log(MAX_DISTANCE / MAX_EXACT))
                         * np.float32(NUM_BUCKETS - MAX_EXACT)).astype(np.int32)
    large = np.minimum(large, NUM_BUCKETS - 1)
    return np.where(d < MAX_EXACT, d, large).astype(np.int32)


def _nsa_constants(rel_bias, seq):
    tq = Q_TILE
    max_d = NEAR_KEYS
    bucket = _t5_bucket_table(max_d)
    far_cmp = tq - (CMP_BLOCK - 1) + CMP_STRIDE
    assert np.all(bucket[min(far_cmp, tq + 1):] == NUM_BUCKETS - 1)
    table = rel_bias.astype(F32).T * LOG2_E
    delta = table[:, bucket] - table[:, NUM_BUCKETS - 1:]

    def toeplitz(d, ok):
        vals = delta[:, np.clip(d, 0, max_d - 1)]
        return jnp.where(jnp.asarray(ok)[None], vals, NEG_INF)

    t = np.arange(tq)[:, None]
    c = np.arange(NEAR_KEYS)[None, :]
    d_near = t + tq - c
    bnear = toeplitz(d_near, d_near >= 0)
    npr = np.arange(16)[None, :]
    d_cmp = t - CMP_STRIDE * npr + (8 * CMP_STRIDE - (CMP_BLOCK - 1))
    bc = jnp.where(jnp.asarray(d_cmp >= 0)[None], delta[:, np.clip(d_cmp, 0, max_d - 1)], 0.0)
    hi = bc.astype(BF16)
    mid = (bc - hi.astype(F32)).astype(BF16)
    lo = (bc - hi.astype(F32) - mid.astype(F32)).astype(BF16)
    bcmp = jnp.concatenate([hi, mid, lo, jnp.zeros((N_HEADS, tq, LANES - 48), BF16)], axis=-1)

    nsb = seq // SEL_BLOCK
    emat = np.zeros((LANES, seq), np.float32)
    emat[np.arange(seq) // SEL_BLOCK, np.arange(seq)] = 1.0
    ns = seq // CMP_STRIDE
    c_start = np.arange(ns) * CMP_STRIDE
    s_start = np.arange(nsb) * SEL_BLOCK
    ov = np.minimum(c_start[:, None] + CMP_BLOCK, s_start[None, :] + SEL_BLOCK) - np.maximum(c_start[:, None], s_start[None, :])
    ov = np.clip(ov, 0, None) // CMP_STRIDE
    selmap_t = np.zeros((LANES, ns), np.float32)
    selmap_t[:nsb, :ns - 1] = ov[:ns - 1].T
    return bnear, bcmp, jnp.asarray(emat, BF16), jnp.asarray(selmap_t, BF16)


def _conv_kernel(cur_ref, prev_ref, w_ref, b_ref, g_ref, beta_ref, o_ref, shift_ref, conv_ref):
    tt, c = cur_ref.shape[1], cur_ref.shape[2]
    first = pl.program_id(1) == 0
    base = CONV_HALO - (CONV_WIDTH - 1)
    groups = CONV_ROW_BLOCK // SUBLANES

    def chunk_body(ci, carry):
        lanes = pl.ds(pl.multiple_of(ci * CONV_LANE_CHUNK, CONV_LANE_CHUNK), CONV_LANE_CHUNK)
        halo = jnp.where(first, 0.0, prev_ref[0, :, lanes])
        xin = jnp.concatenate([halo, cur_ref[0, :, lanes]], axis=0)
        shift_ref[0] = xin
        keep = tt + CONV_HALO - SUBLANES
        for s in range(1, SUBLANES):
            shift_ref[s, 0:keep, :] = xin[s:s + keep]
        bias = b_ref[:, lanes]
        for r0 in range(0, tt, CONV_ROW_BLOCK):
            acc = jnp.zeros((groups, SUBLANES, CONV_LANE_CHUNK), F32) + bias
            for k in range(CONV_WIDTH):
                s = (base + k) % SUBLANES
                a = r0 + base + k - s
                win = shift_ref[s, a:a + CONV_ROW_BLOCK, :].reshape(groups, SUBLANES, CONV_LANE_CHUNK)
                acc = acc + w_ref[k, :, lanes][None] * win
            conv_ref[r0:r0 + CONV_ROW_BLOCK, lanes] = acc.reshape(CONV_ROW_BLOCK, CONV_LANE_CHUNK)
        return carry

    lax.fori_loop(0, c // CONV_LANE_CHUNK, chunk_body, 0)
    acc = conv_ref[...]
    mu = jnp.mean(acc, axis=-1, keepdims=True)
    xc = acc - mu
    var = jnp.mean(xc * xc, axis=-1, keepdims=True)
    y = xc * lax.rsqrt(var + EPS) * g_ref[...] + beta_ref[...]
    o_ref[0] = jax.nn.silu(y).astype(o_ref.dtype)


def _conformer_conv(h, w_dw, b_dw, ln_g, ln_b, tt=256):
    b, t, c = h.shape
    tt = min(tt, t)
    hb = tt // CONV_HALO
    assert c % CONV_LANE_CHUNK == 0 and tt % CONV_ROW_BLOCK == 0 and t % tt == 0
    wrep = jnp.broadcast_to(w_dw.astype(F32)[:, None, :], (CONV_WIDTH, SUBLANES, c))
    vec = lambda a: a.reshape(1, c).astype(F32)
    return pl.pallas_call(
        _conv_kernel,
        grid=(b, t // tt),
        in_specs=[pl.BlockSpec((1, tt, c), lambda bi, ti: (bi, ti, 0)),
                  pl.BlockSpec((1, CONV_HALO, c), lambda bi, ti: (bi, jnp.maximum(ti * hb - 1, 0), 0)),
                  pl.BlockSpec((CONV_WIDTH, SUBLANES, c), lambda bi, ti: (0, 0, 0)),
                  pl.BlockSpec((1, c), lambda bi, ti: (0, 0)),
                  pl.BlockSpec((1, c), lambda bi, ti: (0, 0)),
                  pl.BlockSpec((1, c), lambda bi, ti: (0, 0))],
        out_specs=pl.BlockSpec((1, tt, c), lambda bi, ti: (bi, ti, 0)),
        out_shape=jax.ShapeDtypeStruct((b, t, c), BF16),
        scratch_shapes=[pltpu.VMEM((SUBLANES, CONV_HALO + tt, CONV_LANE_CHUNK), F32),
                        pltpu.VMEM((tt, c), F32)],
        compiler_params=_params("parallel", "arbitrary"),
        name="conformer_conv",
    )(h, h, wrep, vec(b_dw), vec(ln_g), vec(ln_b))


def _nsa(h, w_in_l, cmp_pos, cmp_w1, cmp_w2, consts, batch, seq):
    dh = HEAD_DIM
    g = N_KV_GROUPS
    attn_w = N_HEADS * dh
    kv_w = g * dh
    m = h.shape[0]
    c0 = attn_w
    wb = lambda a, b: w_in_l[:, a:b].astype(BF16)
    q = _mm_call(functools.partial(_mm_scale_kernel, scale=dh ** -0.5 * LOG2_E), [h], [wb(0, attn_w)], [], [],
                 BF16, 1024, 512, "proj_q")
    kvc = _mm_call(functools.partial(_mm_scale_kernel, scale=1.0), [h], [wb(c0, c0 + 2 * kv_w)], [], [],
                   F32, 1024, 512, "proj_kv_cmp")
    kvr = _mm_call(functools.partial(_mm_scale_kernel, scale=1.0), [h], [wb(c0 + 2 * kv_w, c0 + 6 * kv_w)], [], [],
                   BF16, 1024, 512, "proj_kv")
    n_gates = 3 * N_HEADS
    w_br = jnp.pad(w_in_l[:, -n_gates:], ((0, 0), (0, LANES - n_gates))).astype(BF16)
    br = _mm_call(functools.partial(_mm_scale_kernel, scale=1.0), [h], [w_br], [], [],
                  F32, 1024, LANES, "proj_branch_gates")

    ns = seq // CMP_STRIDE
    xkv = kvc.reshape(batch, ns, CMP_STRIDE, 2, g, dh).transpose(3, 0, 4, 1, 2, 5).reshape(2, batch * g, ns, CMP_STRIDE * dh)
    pos = cmp_pos.astype(F32).reshape(2, 2, 1, CMP_STRIDE * dh)
    w1 = cmp_w1.astype(BF16).reshape(2, 2, CMP_STRIDE * dh, dh)
    kvcmp = _compress(xkv, pos, w1, cmp_w2.astype(BF16))

    kvr = kvr.reshape(batch, seq, 4 * kv_w)
    padt = lambda a, n: jnp.pad(a, ((0, 0), (n, 0), (0, 0)))

    def with_ones(v):
        v = v.reshape(batch, seq, g, dh)
        return jnp.concatenate([v, jnp.ones_like(v)], axis=-1).reshape(batch, seq, 2 * kv_w)

    kslc = padt(kvr[:, :, 0:kv_w], Q_TILE)
    vslc = padt(with_ones(kvr[:, :, kv_w:2 * kv_w]), Q_TILE)
    kwin = padt(kvr[:, :, 2 * kv_w:3 * kv_w], WINDOW)
    vwin = padt(with_ones(kvr[:, :, 3 * kv_w:4 * kv_w]), WINDOW)
    per_g = 3 * HEADS_PER_GROUP
    gates = br[:, :n_gates].reshape(batch, seq, g, per_g).transpose(0, 2, 1, 3)
    gates = jnp.pad(gates, ((0, 0), (0, 0), (0, 0), (0, LANES - per_g)))
    o = _nsa_attention(q.reshape(batch, seq, attn_w), kslc, vslc, kwin, vwin, kvcmp, gates, *consts)
    return o.reshape(m, attn_w)


def kernel(x, w_in, cmp_pos, cmp_w1, cmp_w2, rel_bias, w_attn_out, b_glu, w_dw, b_dw, conv_ln_g, conv_ln_b, w_conv_out, b_conv_out, w_out, norm_mix, norm_ffn, w_ffn_gate, w_ffn_up, w_ffn_down, norm_final):
    batch, seq, d = x.shape
    depth = w_in.shape[0]
    m = batch * seq
    attn_w = N_HEADS * HEAD_DIM
    kv_w = N_KV_GROUPS * HEAD_DIM
    conv_ch = w_dw.shape[-1]
    c_conv = attn_w + 6 * kv_w
    c_gate = c_conv + 2 * conv_ch
    consts = _nsa_constants(rel_bias, seq)
    xf = x.reshape(m, d).astype(F32)
    row = lambda a: a.reshape(1, -1).astype(F32)
    for l in range(depth):
        w_in_l = w_in[l]
        h = _rmsnorm(xf, norm_mix[l], BF16)
        attn = _nsa(h, w_in_l, cmp_pos[l], cmp_w1[l], cmp_w2[l], consts, batch, seq)
        glu = _mm_call(_mm_glu_kernel, [h],
                       [w_in_l[:, c_conv:c_conv + conv_ch].astype(BF16),
                        w_in_l[:, c_conv + conv_ch:c_gate].astype(BF16)],
                       [], [row(b_glu[l, :conv_ch]), row(b_glu[l, conv_ch:])], F32, 1024, 512, "proj_conv_glu")
        conv = _conformer_conv(glu.reshape(batch, seq, conv_ch), w_dw[l], b_dw[l], conv_ln_g[l], conv_ln_b[l])
        gate_ab = _mm_call(functools.partial(_mm_scale_kernel, scale=1.0), [h],
                           [w_in_l[:, c_gate:c_gate + 2 * d].astype(BF16)], [], [], F32, 1024, 512, "proj_merge_gates")
        merged = _mm_call(_mm_merge_kernel, [attn, conv.reshape(m, conv_ch)],
                          [w_attn_out[l].astype(BF16), w_conv_out[l].astype(BF16)],
                          [(gate_ab, 0), (gate_ab, d)], [row(b_conv_out[l])], BF16, 512, 512, "merge")
        xf = _mm_call(_mm_residual_kernel, [merged], [w_out[l].astype(BF16)], [(xf, 0)], [], F32, 1024, 512, "mix_out")
        h = _rmsnorm(xf, norm_ffn[l], BF16)
        gu = _mm_call(_mm_swiglu_kernel, [h], [w_ffn_gate[l].astype(BF16), w_ffn_up[l].astype(BF16)],
                      [], [], BF16, 1024, 256, "ffn_swiglu")
        xf = _mm_call(_mm_residual_kernel, [gu], [w_ffn_down[l].astype(BF16)], [(xf, 0)], [], F32, 512, 256, "ffn_down")
    out = _rmsnorm(xf, norm_final, F32)
    return out.reshape(batch, seq, d).astype(x.dtype)
```

```python
import functools
import math

import jax
import jax.numpy as jnp
import numpy as np
from jax import lax
from jax.experimental import pallas as pl
from jax.experimental.pallas import tpu as pltpu

F32 = jnp.float32
BF16 = jnp.bfloat16

N_HEADS = 32
HEAD_DIM = 128
N_KV_GROUPS = 4
HEADS_PER_GROUP = N_HEADS // N_KV_GROUPS
CMP_BLOCK = 32
CMP_STRIDE = 16
SEL_BLOCK = 64
LOG2_SEL_BLOCK = 6
SEL_TOPK = 16
WINDOW = 512
CONV_WIDTH = 31
NUM_BUCKETS = 32
MAX_EXACT = NUM_BUCKETS // 2
MAX_DISTANCE = 128
EPS = 1e-6
NEG_INF = -1e30

LANES = 128
SUBLANES = 8
VMEM_LIMIT_BYTES = 56 * 1024 * 1024
MM_VMEM_BUDGET_BYTES = 48 * 1024 * 1024
LOG2_E = math.log2(math.e)

Q_TILE = 128
FAR_CHUNK = 512
NEAR_KEYS = 2 * Q_TILE
WIN_KEYS = WINDOW + Q_TILE
CMP_HEADS = 8
FAR_HEADS = 1
NEAR_HEADS = 8
WIN_HEADS = 4
CONV_HALO = 32
CONV_LANE_CHUNK = 512
CONV_ROW_BLOCK = 32


def _params(*sem):
    return pltpu.CompilerParams(dimension_semantics=sem, vmem_limit_bytes=VMEM_LIMIT_BYTES)


def _rmsnorm_kernel(x_ref, g_ref, o_ref):
    x = x_ref[...]
    ms = jnp.mean(x * x, axis=-1, keepdims=True)
    o_ref[...] = (x * lax.rsqrt(ms + EPS) * g_ref[...]).astype(o_ref.dtype)


def _rmsnorm(x, g, out_dtype, tm=256):
    m, d = x.shape
    return pl.pallas_call(
        _rmsnorm_kernel,
        grid=(m // tm,),
        in_specs=[pl.BlockSpec((tm, d), lambda i: (i, 0)),
                  pl.BlockSpec((1, d), lambda i: (0, 0))],
        out_specs=pl.BlockSpec((tm, d), lambda i: (i, 0)),
        out_shape=jax.ShapeDtypeStruct((m, d), out_dtype),
        compiler_params=_params("parallel"),
        name="rmsnorm",
    )(x, g.reshape(1, d).astype(F32))


def _cast_kernel(x_ref, o_ref):
    o_ref[...] = x_ref[...].astype(o_ref.dtype)


def _cast_bf16(w, tr=128):
    r, c = w.shape
    assert r % tr == 0
    return pl.pallas_call(
        _cast_kernel,
        grid=(r // tr,),
        in_specs=[pl.BlockSpec((tr, c), lambda i: (i, 0))],
        out_specs=pl.BlockSpec((tr, c), lambda i: (i, 0)),
        out_shape=jax.ShapeDtypeStruct((r, c), BF16),
        compiler_params=_params("parallel"),
        name="cast_bf16",
    )(w)


def _dot(a, b):
    return jnp.dot(a, b, preferred_element_type=F32)


def _mm_scale_kernel(a_ref, b_ref, o_ref, *, scale):
    acc = _dot(a_ref[...], b_ref[...])
    if scale != 1.0:
        acc = acc * scale
    o_ref[...] = acc.astype(o_ref.dtype)


def _mm_glu_kernel(a_ref, b1_ref, b2_ref, v1_ref, v2_ref, o_ref):
    a = a_ref[...]
    u = _dot(a, b1_ref[...]) + v1_ref[...]
    g = _dot(a, b2_ref[...]) + v2_ref[...]
    o_ref[...] = (u * jax.nn.sigmoid(g)).astype(o_ref.dtype)


def _mm_swiglu_kernel(a_ref, b1_ref, b2_ref, o_ref):
    a = a_ref[...]
    g = _dot(a, b1_ref[...])
    u = _dot(a, b2_ref[...])
    o_ref[...] = (jax.nn.silu(g) * u).astype(o_ref.dtype)


def _mm_merge_kernel(a1_ref, a2_ref, b1_ref, b2_ref, ga_ref, gb_ref, v_ref, o_ref):
    ya = _dot(a1_ref[...], b1_ref[...])
    yb = _dot(a2_ref[...], b2_ref[...]) + v_ref[...]
    o_ref[...] = (jax.nn.sigmoid(ga_ref[...]) * ya + jax.nn.sigmoid(gb_ref[...]) * yb).astype(o_ref.dtype)


def _mm_residual_kernel(a_ref, b_ref, r_ref, o_ref):
    o_ref[...] = r_ref[...] + _dot(a_ref[...], b_ref[...])


def _mm_tiles(m, n, a_ks, b_ks, n_mn, out_itemsize):
    bf16_bytes, f32_bytes, double = 2, 4, 2
    best = None
    for tm in (2048, 1024, 512, 256, 128):
        if m % tm:
            continue
        for tn in (1024, 512, 256, 128):
            if n % tn:
                continue
            fixed = (sum(k * tn * bf16_bytes * double for k in b_ks)
                     + n_mn * tm * tn * f32_bytes * double
                     + tm * tn * out_itemsize * double
                     + (len(b_ks) + 1) * tm * tn * f32_bytes)
            a_once = sum(tm * k * bf16_bytes for k in a_ks)
            for a_bufs in (2, 1):
                if fixed + a_bufs * a_once <= MM_VMEM_BUDGET_BYTES:
                    key = ((m // tm) * (n // tn), -a_bufs, -tm)
                    if best is None or key < best[0]:
                        best = (key, tm, tn, a_bufs)
                    break
    assert best is not None, (m, n, a_ks, b_ks)
    return best[1:]


def _panel(w, row_block=0, col0=0):
    return (w, row_block, col0)


def _mm_call(body, a_ops, b_ops, mn_ops, vec_ops, out_dtype, name, n=None):
    m = a_ops[0].shape[0]
    n = b_ops[0][0].shape[1] if n is None else n
    a_ks = [a.shape[1] for a in a_ops]
    b_ks = [a_ks[min(idx, len(a_ks) - 1)] for idx in range(len(b_ops))]
    tm, tn, a_bufs = _mm_tiles(m, n, a_ks, b_ks, len(mn_ops), jnp.dtype(out_dtype).itemsize)
    a_mode = {} if a_bufs == 2 else dict(pipeline_mode=pl.Buffered(1))
    in_specs = []
    for a in a_ops:
        in_specs.append(pl.BlockSpec((tm, a.shape[1]), lambda i, j: (i, 0), **a_mode))
    for (w, row_block, col0), k in zip(b_ops, b_ks):
        assert col0 % tn == 0 and w.shape[0] % k == 0 and col0 + n <= w.shape[1]
        in_specs.append(pl.BlockSpec((k, tn), lambda i, j, rb=row_block, off=col0 // tn: (rb, j + off)))
    b_ops = [w for w, _, _ in b_ops]
    for _, col0 in mn_ops:
        assert col0 % tn == 0
        in_specs.append(pl.BlockSpec((tm, tn), lambda i, j, off=col0 // tn: (i, j + off)))
    mn_ops = [a for a, _ in mn_ops]
    for _ in vec_ops:
        in_specs.append(pl.BlockSpec((1, tn), lambda i, j: (0, j)))
    return pl.pallas_call(
        body,
        grid=(m // tm, n // tn),
        in_specs=in_specs,
        out_specs=pl.BlockSpec((tm, tn), lambda i, j: (i, j)),
        out_shape=jax.ShapeDtypeStruct((m, n), out_dtype),
        compiler_params=_params("parallel", "arbitrary"),
        name=name,
    )(*a_ops, *b_ops, *mn_ops, *vec_ops)


def _gelu_tanh(x):
    c = math.sqrt(2.0 / math.pi)
    return x * (0.5 * (1.0 + jnp.tanh(c * (x + 0.044715 * (x * x * x)))))


def _compress_kernel(x_ref, pos_ref, w1_ref, w2_ref, o_ref):
    x = x_ref[0, 0]
    p1 = _dot((x + pos_ref[0, 0]).astype(BF16), w1_ref[0, 0])
    p2 = _dot((x + pos_ref[0, 1]).astype(BF16), w1_ref[0, 1])
    ns = x.shape[0]
    pre = p1 + pltpu.roll(p2, ns - 1, 0)
    h = _gelu_tanh(pre).astype(BF16)
    o_ref[0, 0] = _dot(h, w2_ref[0]).astype(o_ref.dtype)


def _compress(xkv, pos, w1, w2):
    _, bg, ns, kw = xkv.shape
    dh = w2.shape[-1]
    return pl.pallas_call(
        _compress_kernel,
        grid=(2, bg),
        in_specs=[pl.BlockSpec((1, 1, ns, kw), lambda k, i: (k, i, 0, 0)),
                  pl.BlockSpec((1, 2, 1, kw), lambda k, i: (k, 0, 0, 0)),
                  pl.BlockSpec((1, 2, kw, dh), lambda k, i: (k, 0, 0, 0)),
                  pl.BlockSpec((1, dh, dh), lambda k, i: (k, 0, 0))],
        out_specs=pl.BlockSpec((1, 1, ns, dh), lambda k, i: (k, i, 0, 0)),
        out_shape=jax.ShapeDtypeStruct((2, bg, ns, dh), BF16),
        compiler_params=_params("parallel", "parallel"),
        name="nsa_compress",
    )(xkv, pos, w1, w2)


def _nt_dot(a, b):
    return lax.dot_general(a, b, (((1,), (1,)), ((), ())), preferred_element_type=F32)


def _nsa_kernel(q_ref, kslc_ref, vslc_ref, kwin_ref, vwin_ref, kcmp_ref, vcmp_ref, gates_ref,
                bnear_ref, bcmp_ref, emat_ref, selmap_t_ref, o_ref,
                m_ref, acc_ref):
    i = pl.program_id(2)
    tq = Q_TILE
    nh = HEADS_PER_GROUP
    dh = HEAD_DIM
    rows = nh * tq
    n_cmp = kcmp_ref.shape[2]

    def q_block(r0, hb):
        return jnp.concatenate([q_ref[0, :, r * dh:(r + 1) * dh] for r in range(r0, r0 + hb)], axis=0)

    def stack_heads(ref, r0, hb):
        return ref[r0:r0 + hb].reshape(hb * tq, ref.shape[-1])

    def block_rows(r0, hb):
        return slice(r0 * tq, (r0 + hb) * tq)

    def add_shared(s, shared):
        hb, n = s.shape[0] // tq, s.shape[1]
        return (s.reshape(hb, tq, n) + shared[None]).reshape(hb * tq, n)

    lane = lax.broadcasted_iota(jnp.int32, (tq, LANES), 1)

    nrow = lax.broadcasted_iota(jnp.int32, (n_cmp, LANES), 0)
    ncol = lax.broadcasted_iota(jnp.int32, (n_cmp, LANES), 1)
    place = jnp.where((nrow == (ncol & 15) + (8 * i - 8)) & (ncol < 48), 1.0, 0.0).astype(BF16)
    k_aug = jnp.concatenate([kcmp_ref[0, 0], place], axis=1)
    tc = lax.broadcasted_iota(jnp.int32, (tq, n_cmp), 0) + i * tq
    nc = lax.broadcasted_iota(jnp.int32, (tq, n_cmp), 1)
    ok_c = (tc - (nc * CMP_STRIDE + CMP_BLOCK - 1)) >= 0
    mask_c = jnp.where(ok_c, 0.0, NEG_INF)
    v_cmp = vcmp_ref[0, 0]
    o_cmp = []
    psum = jnp.zeros((tq, n_cmp), F32)
    for r0 in range(0, nh, CMP_HEADS):
        q_aug = jnp.concatenate([q_block(r0, CMP_HEADS), stack_heads(bcmp_ref, r0, CMP_HEADS)], axis=1)
        s = add_shared(_nt_dot(q_aug, k_aug), mask_c)
        e = jnp.exp2(s - jnp.max(s, axis=-1, keepdims=True))
        p = e / jnp.sum(e, axis=-1, keepdims=True)
        p3 = jnp.where(ok_c[None], p.reshape(CMP_HEADS, tq, n_cmp), 0.0)
        o_cmp.append(_dot(p3.reshape(CMP_HEADS * tq, n_cmp).astype(BF16), v_cmp))
        psum = psum + jnp.sum(p3, axis=0)
    o_cmp = jnp.concatenate(o_cmp, axis=0)

    p_hi = psum.astype(BF16)
    p_lo = (psum - p_hi.astype(F32)).astype(BF16)
    imp_t = _nt_dot(selmap_t_ref[...], p_hi) + _nt_dot(selmap_t_ref[...], p_lo)
    n_blocks = emat_ref.shape[1] // SEL_BLOCK
    assert n_blocks <= LANES and n_blocks % SUBLANES == 0
    blk = lax.broadcasted_iota(jnp.int32, (n_blocks, tq), 0)
    cur_t = (lax.broadcasted_iota(jnp.int32, (n_blocks, tq), 1) + i * tq) >> LOG2_SEL_BLOCK
    valid_t = blk <= cur_t
    forced_t = (blk == 0) | (blk == cur_t) | (blk == cur_t - 1)
    big = 1e30
    x = jnp.where(valid_t, jnp.where(forced_t, big, imp_t[:n_blocks]), -big)
    bcast = [jnp.broadcast_to(x[j:j + 1], (SUBLANES, tq)) for j in range(n_blocks)]
    cnt_groups = []
    for g0 in range(0, n_blocks, SUBLANES):
        xg = x[g0:g0 + SUBLANES]
        blk_g = blk[g0:g0 + SUBLANES]
        cnt = jnp.zeros((SUBLANES, tq), F32)
        for j in range(n_blocks):
            if j < g0:
                beats = bcast[j] >= xg
            elif j >= g0 + SUBLANES:
                beats = bcast[j] > xg
            else:
                beats = (bcast[j] > xg) | ((bcast[j] == xg) & (blk_g > j))
            cnt = cnt + jnp.where(beats, 1.0, 0.0)
        cnt_groups.append(cnt)
    cnt = jnp.concatenate(cnt_groups, axis=0)
    sel_t = jnp.where(valid_t & (cnt < SEL_TOPK), 1.0, 0.0)
    sel_t = jnp.concatenate([sel_t, jnp.zeros((LANES - n_blocks, tq), F32)], axis=0)
    sel = sel_t.T

    def lane_tile(x, n):
        return x if n == 1 else jnp.concatenate([x] * n, axis=1)

    def with_ones(v):
        return jnp.concatenate([v, jnp.ones_like(v)], axis=1)

    m_ref[...] = jnp.full(m_ref.shape, NEG_INF, F32)
    acc_ref[...] = jnp.zeros(acc_ref.shape, F32)

    def online_update(sl, s, v_ones):
        m_prev = m_ref[sl]
        m_next = jnp.maximum(m_prev, jnp.max(s, axis=-1, keepdims=True))
        alpha = jnp.exp2(m_prev - m_next)
        p = jnp.exp2(s - lane_tile(m_next, s.shape[-1] // LANES))
        acc_ref[sl] = lane_tile(alpha, 2) * acc_ref[sl] + _dot(p.astype(BF16), v_ones)
        m_ref[sl] = m_next

    sel_far = jnp.where(lane <= 2 * i - 3, sel, 0.0).astype(BF16)

    def far_body(c, carry):
        c0 = pl.multiple_of(c * FAR_CHUNK, FAR_CHUNK)
        k = kslc_ref[0, pl.ds(c0 + Q_TILE, FAR_CHUNK), :]
        v = with_ones(vslc_ref[0, pl.ds(c0 + Q_TILE, FAR_CHUNK), :])
        hit = _dot(sel_far, emat_ref[:, pl.ds(c0, FAR_CHUNK)])
        mask = (hit - 1.0) * (-NEG_INF)
        for h0 in range(0, nh, FAR_HEADS):
            online_update(block_rows(h0, FAR_HEADS), add_shared(_nt_dot(q_block(h0, FAR_HEADS), k), mask), v)
        return carry

    tiles_per_chunk = FAR_CHUNK // Q_TILE
    lax.fori_loop(0, (i + tiles_per_chunk - 2) // tiles_per_chunk, far_body, 0)

    r0 = pl.multiple_of(i * tq, tq)
    k = kslc_ref[0, pl.ds(r0, NEAR_KEYS), :]
    v = with_ones(vslc_ref[0, pl.ds(r0, NEAR_KEYS), :])
    erow = lax.broadcasted_iota(jnp.int32, (LANES, NEAR_KEYS), 0)
    ecol = lax.broadcasted_iota(jnp.int32, (LANES, NEAR_KEYS), 1)
    e_near = jnp.where(erow == 2 * i - 2 + (ecol >> LOG2_SEL_BLOCK), 1.0, 0.0).astype(BF16)
    hit = _dot(sel.astype(BF16), e_near)
    mask = (hit - 1.0) * (-NEG_INF)
    for h0 in range(0, nh, NEAR_HEADS):
        s = add_shared(_nt_dot(q_block(h0, NEAR_HEADS), k), mask) + stack_heads(bnear_ref, h0, NEAR_HEADS)
        online_update(block_rows(h0, NEAR_HEADS), s, v)

    k_w = kwin_ref[0, pl.ds(r0, WIN_KEYS), :]
    v_w = with_ones(vwin_ref[0, pl.ds(r0, WIN_KEYS), :])
    wrow = lax.broadcasted_iota(jnp.int32, (tq, WINDOW), 0)
    wcol = lax.broadcasted_iota(jnp.int32, (tq, WINDOW), 1)
    ok_w = (wcol >= WINDOW - i * tq) & ((wcol >= tq) | (wcol > wrow))
    mask_w = jnp.where(ok_w, 0.0, NEG_INF)
    old = WIN_KEYS - NEAR_KEYS
    gates = jax.nn.sigmoid(gates_ref[0, 0])
    for h0 in range(0, nh, WIN_HEADS):
        s = _nt_dot(q_block(h0, WIN_HEADS), k_w)
        bias = stack_heads(bnear_ref, h0, WIN_HEADS)
        s = jnp.concatenate([add_shared(s[:, :old], mask_w[:, :old]),
                             add_shared(s[:, old:WINDOW], mask_w[:, old:]) + bias[:, :WINDOW - old],
                             s[:, WINDOW:] + bias[:, WINDOW - old:]], axis=1)
        pv = _dot(jnp.exp2(s - jnp.max(s, axis=-1, keepdims=True)).astype(BF16), v_w)
        o_win = pv[:, :dh] / pv[:, dh:]
        acc = acc_ref[block_rows(h0, WIN_HEADS)]
        o_slc = acc[:, :dh] / acc[:, dh:]
        for r in range(h0, h0 + WIN_HEADS):
            sl = block_rows(r - h0, 1)
            o = (gates[:, 3 * r:3 * r + 1] * o_cmp[block_rows(r, 1)]
                 + gates[:, 3 * r + 1:3 * r + 2] * o_slc[sl]
                 + gates[:, 3 * r + 2:3 * r + 3] * o_win[sl])
            o_ref[0, :, r * dh:(r + 1) * dh] = o.astype(o_ref.dtype)


def _nsa_attention(q, kslc, vslc, kwin, vwin, kvcmp, gates, bnear, bcmp, emat, selmap_t):
    b, t, _ = q.shape
    g = N_KV_GROUPS
    nh = HEADS_PER_GROUP
    dh = HEAD_DIM
    tq = Q_TILE
    n_cmp = kvcmp.shape[2]
    tp_slc = kslc.shape[1]
    tp_win = kwin.shape[1]
    rows = nh * tq
    kv_spec = lambda tp, w: pl.BlockSpec((1, tp, w), lambda bi, gi, ti: (bi, 0, gi))
    return pl.pallas_call(
        _nsa_kernel,
        grid=(b, g, t // tq),
        in_specs=[
            pl.BlockSpec((1, tq, nh * dh), lambda bi, gi, ti: (bi, ti, gi)),
            kv_spec(tp_slc, dh), kv_spec(tp_slc, dh), kv_spec(tp_win, dh), kv_spec(tp_win, dh),
            pl.BlockSpec((1, 1, n_cmp, dh), lambda bi, gi, ti: (0, bi * N_KV_GROUPS + gi, 0, 0)),
            pl.BlockSpec((1, 1, n_cmp, dh), lambda bi, gi, ti: (1, bi * N_KV_GROUPS + gi, 0, 0)),
            pl.BlockSpec((1, 1, tq, LANES), lambda bi, gi, ti: (bi, gi, ti, 0)),
            pl.BlockSpec((nh, tq, NEAR_KEYS), lambda bi, gi, ti: (gi, 0, 0)),
            pl.BlockSpec((nh, tq, LANES), lambda bi, gi, ti: (gi, 0, 0)),
            pl.BlockSpec(emat.shape, lambda bi, gi, ti: (0, 0)),
            pl.BlockSpec(selmap_t.shape, lambda bi, gi, ti: (0, 0)),
        ],
        out_specs=pl.BlockSpec((1, tq, nh * dh), lambda bi, gi, ti: (bi, ti, gi)),
        out_shape=jax.ShapeDtypeStruct((b, t, g * nh * dh), BF16),
        scratch_shapes=[pltpu.VMEM((rows, LANES), F32),
                        pltpu.VMEM((rows, 2 * dh), F32)],
        compiler_params=_params("parallel", "parallel", "arbitrary"),
        name="nsa_attention",
    )(q, kslc, vslc, kwin, vwin, kvcmp, kvcmp, gates, bnear, bcmp, emat, selmap_t)


def _t5_bucket_table(max_dist):
    d = np.arange(max_dist)
    nf = np.maximum(d, MAX_EXACT).astype(np.float32)
    large = MAX_EXACT + (np.log(nf / np.float32(MAX_EXACT)) / np.float32(math.log(MAX_DISTANCE / MAX_EXACT))
                         * np.float32(NUM_BUCKETS - MAX_EXACT)).astype(np.int32)
    large = np.minimum(large, NUM_BUCKETS - 1)
    return np.where(d < MAX_EXACT, d, large).astype(np.int32)


def _nsa_constants(rel_bias, seq):
    tq = Q_TILE
    max_d = NEAR_KEYS
    bucket = _t5_bucket_table(max_d)
    far_cmp = tq - (CMP_BLOCK - 1) + CMP_STRIDE
    assert np.all(bucket[min(far_cmp, tq + 1):] == NUM_BUCKETS - 1)
    table = rel_bias.astype(F32).T * LOG2_E
    delta = table[:, bucket] - table[:, NUM_BUCKETS - 1:]

    def toeplitz(d, ok):
        vals = delta[:, np.clip(d, 0, max_d - 1)]
        return jnp.where(jnp.asarray(ok)[None], vals, NEG_INF)

    t = np.arange(tq)[:, None]
    c = np.arange(NEAR_KEYS)[None, :]
    d_near = t + tq - c
    bnear = toeplitz(d_near, d_near >= 0)
    npr = np.arange(16)[None, :]
    d_cmp = t - CMP_STRIDE * npr + (8 * CMP_STRIDE - (CMP_BLOCK - 1))
    bc = jnp.where(jnp.asarray(d_cmp >= 0)[None], delta[:, np.clip(d_cmp, 0, max_d - 1)], 0.0)
    hi = bc.astype(BF16)
    mid = (bc - hi.astype(F32)).astype(BF16)
    lo = (bc - hi.astype(F32) - mid.astype(F32)).astype(BF16)
    bcmp = jnp.concatenate([hi, mid, lo, jnp.zeros((N_HEADS, tq, LANES - 48), BF16)], axis=-1)

    nsb = seq // SEL_BLOCK
    emat = np.zeros((LANES, seq), np.float32)
    emat[np.arange(seq) // SEL_BLOCK, np.arange(seq)] = 1.0
    ns = seq // CMP_STRIDE
    c_start = np.arange(ns) * CMP_STRIDE
    s_start = np.arange(nsb) * SEL_BLOCK
    ov = np.minimum(c_start[:, None] + CMP_BLOCK, s_start[None, :] + SEL_BLOCK) - np.maximum(c_start[:, None], s_start[None, :])
    ov = np.clip(ov, 0, None) // CMP_STRIDE
    selmap_t = np.zeros((LANES, ns), np.float32)
    selmap_t[:nsb, :ns - 1] = ov[:ns - 1].T
    return bnear, bcmp, jnp.asarray(emat, BF16), jnp.asarray(selmap_t, BF16)


def _conv_kernel(cur_ref, prev_ref, w_ref, b_ref, g_ref, beta_ref, o_ref, shift_ref, conv_ref):
    tt, c = cur_ref.shape[1], cur_ref.shape[2]
    first = pl.program_id(1) == 0
    base = CONV_HALO - (CONV_WIDTH - 1)
    groups = CONV_ROW_BLOCK // SUBLANES

    def chunk_body(ci, carry):
        lanes = pl.ds(pl.multiple_of(ci * CONV_LANE_CHUNK, CONV_LANE_CHUNK), CONV_LANE_CHUNK)
        halo = jnp.where(first, 0.0, prev_ref[0, :, lanes])
        xin = jnp.concatenate([halo, cur_ref[0, :, lanes]], axis=0)
        shift_ref[0] = xin
        keep = tt + CONV_HALO - SUBLANES
        for s in range(1, SUBLANES):
            shift_ref[s, 0:keep, :] = xin[s:s + keep]
        bias = b_ref[:, lanes]
        for r0 in range(0, tt, CONV_ROW_BLOCK):
            acc = jnp.zeros((groups, SUBLANES, CONV_LANE_CHUNK), F32) + bias
            for k in range(CONV_WIDTH):
                s = (base + k) % SUBLANES
                a = r0 + base + k - s
                win = shift_ref[s, a:a + CONV_ROW_BLOCK, :].reshape(groups, SUBLANES, CONV_LANE_CHUNK)
                acc = acc + w_ref[k, :, lanes][None] * win
            conv_ref[r0:r0 + CONV_ROW_BLOCK, lanes] = acc.reshape(CONV_ROW_BLOCK, CONV_LANE_CHUNK)
        return carry

    lax.fori_loop(0, c // CONV_LANE_CHUNK, chunk_body, 0)
    acc = conv_ref[...]
    mu = jnp.mean(acc, axis=-1, keepdims=True)
    xc = acc - mu
    var = jnp.mean(xc * xc, axis=-1, keepdims=True)
    y = xc * lax.rsqrt(var + EPS) * g_ref[...] + beta_ref[...]
    o_ref[0] = jax.nn.silu(y).astype(o_ref.dtype)


def _conformer_conv(h, w_dw, b_dw, ln_g, ln_b, tt=256):
    b, t, c = h.shape
    tt = min(tt, t)
    hb = tt // CONV_HALO
    assert c % CONV_LANE_CHUNK == 0 and tt % CONV_ROW_BLOCK == 0 and t % tt == 0
    wrep = jnp.broadcast_to(w_dw.astype(F32)[:, None, :], (CONV_WIDTH, SUBLANES, c))
    vec = lambda a: a.reshape(1, c).astype(F32)
    return pl.pallas_call(
        _conv_kernel,
        grid=(b, t // tt),
        in_specs=[pl.BlockSpec((1, tt, c), lambda bi, ti: (bi, ti, 0)),
                  pl.BlockSpec((1, CONV_HALO, c), lambda bi, ti: (bi, jnp.maximum(ti * hb - 1, 0), 0)),
                  pl.BlockSpec((CONV_WIDTH, SUBLANES, c), lambda bi, ti: (0, 0, 0)),
                  pl.BlockSpec((1, c), lambda bi, ti: (0, 0)),
                  pl.BlockSpec((1, c), lambda bi, ti: (0, 0)),
                  pl.BlockSpec((1, c), lambda bi, ti: (0, 0))],
        out_specs=pl.BlockSpec((1, tt, c), lambda bi, ti: (bi, ti, 0)),
        out_shape=jax.ShapeDtypeStruct((b, t, c), BF16),
        scratch_shapes=[pltpu.VMEM((SUBLANES, CONV_HALO + tt, CONV_LANE_CHUNK), F32),
                        pltpu.VMEM((tt, c), F32)],
        compiler_params=_params("parallel", "arbitrary"),
        name="conformer_conv",
    )(h, h, wrep, vec(b_dw), vec(ln_g), vec(ln_b))


def _nsa(h, w_in_bf, layer, w_in_l, cmp_pos, cmp_w1, cmp_w2, consts, batch, seq):
    dh = HEAD_DIM
    g = N_KV_GROUPS
    attn_w = N_HEADS * dh
    kv_w = g * dh
    m = h.shape[0]
    c0 = attn_w
    plain = functools.partial(_mm_scale_kernel, scale=1.0)
    q = _mm_call(functools.partial(_mm_scale_kernel, scale=dh ** -0.5 * LOG2_E), [h], [_panel(w_in_bf, layer, 0)],
                 [], [], BF16, "proj_q", n=attn_w)
    kvc = _mm_call(plain, [h], [_panel(w_in_bf, layer, c0)], [], [], F32, "proj_kv_cmp", n=2 * kv_w)
    kvr = _mm_call(plain, [h], [_panel(w_in_bf, layer, c0 + 2 * kv_w)], [], [], BF16, "proj_kv", n=4 * kv_w)
    n_gates = 3 * N_HEADS
    w_br = jnp.pad(w_in_l[:, -n_gates:], ((0, 0), (0, LANES - n_gates))).astype(BF16)
    br = _mm_call(plain, [h], [_panel(w_br)], [], [], F32, "proj_branch_gates")

    ns = seq // CMP_STRIDE
    xkv = kvc.reshape(batch, ns, CMP_STRIDE, 2, g, dh).transpose(3, 0, 4, 1, 2, 5).reshape(2, batch * g, ns, CMP_STRIDE * dh)
    pos = cmp_pos.astype(F32).reshape(2, 2, 1, CMP_STRIDE * dh)
    w1 = cmp_w1.astype(BF16).reshape(2, 2, CMP_STRIDE * dh, dh)
    kvcmp = _compress(xkv, pos, w1, cmp_w2.astype(BF16))

    kvr = kvr.reshape(batch, seq, 4 * kv_w)
    padt = lambda a, n: jnp.pad(a, ((0, 0), (n, 0), (0, 0)))
    kslc = padt(kvr[:, :, 0:kv_w], Q_TILE)
    vslc = padt(kvr[:, :, kv_w:2 * kv_w], Q_TILE)
    kwin = padt(kvr[:, :, 2 * kv_w:3 * kv_w], WINDOW)
    vwin = padt(kvr[:, :, 3 * kv_w:4 * kv_w], WINDOW)
    per_g = 3 * HEADS_PER_GROUP
    gates = br[:, :n_gates].reshape(batch, seq, g, per_g).transpose(0, 2, 1, 3)
    gates = jnp.pad(gates, ((0, 0), (0, 0), (0, 0), (0, LANES - per_g)))
    o = _nsa_attention(q.reshape(batch, seq, attn_w), kslc, vslc, kwin, vwin, kvcmp, gates, *consts)
    return o.reshape(m, attn_w)


def kernel(x, w_in, cmp_pos, cmp_w1, cmp_w2, rel_bias, w_attn_out, b_glu, w_dw, b_dw, conv_ln_g, conv_ln_b, w_conv_out, b_conv_out, w_out, norm_mix, norm_ffn, w_ffn_gate, w_ffn_up, w_ffn_down, norm_final):
    batch, seq, d = x.shape
    depth = w_in.shape[0]
    m = batch * seq
    attn_w = N_HEADS * HEAD_DIM
    kv_w = N_KV_GROUPS * HEAD_DIM
    conv_ch = w_dw.shape[-1]
    c_conv = attn_w + 6 * kv_w
    c_gate = c_conv + 2 * conv_ch
    consts = _nsa_constants(rel_bias, seq)
    xf = x.reshape(m, d).astype(F32)
    row = lambda a: a.reshape(1, -1).astype(F32)
    w_in_bf = _cast_bf16(w_in.reshape(depth * d, w_in.shape[-1]))
    wbf = lambda w: _panel(w.astype(BF16))
    for l in range(depth):
        h = _rmsnorm(xf, norm_mix[l], BF16)
        attn = _nsa(h, w_in_bf, l, w_in[l], cmp_pos[l], cmp_w1[l], cmp_w2[l], consts, batch, seq)
        glu = _mm_call(_mm_glu_kernel, [h], [_panel(w_in_bf, l, c_conv), _panel(w_in_bf, l, c_conv + conv_ch)],
                       [], [row(b_glu[l, :conv_ch]), row(b_glu[l, conv_ch:])], F32, "proj_conv_glu", n=conv_ch)
        conv = _conformer_conv(glu.reshape(batch, seq, conv_ch), w_dw[l], b_dw[l], conv_ln_g[l], conv_ln_b[l])
        gate_ab = _mm_call(functools.partial(_mm_scale_kernel, scale=1.0), [h], [_panel(w_in_bf, l, c_gate)],
                           [], [], F32, "proj_merge_gates", n=2 * d)
        merged = _mm_call(_mm_merge_kernel, [attn, conv.reshape(m, conv_ch)],
                          [wbf(w_attn_out[l]), wbf(w_conv_out[l])],
                          [(gate_ab, 0), (gate_ab, d)], [row(b_conv_out[l])], BF16, "merge")
        xf = _mm_call(_mm_residual_kernel, [merged], [wbf(w_out[l])], [(xf, 0)], [], F32, "mix_out")
        h = _rmsnorm(xf, norm_ffn[l], BF16)
        gu = _mm_call(_mm_swiglu_kernel, [h], [wbf(w_ffn_gate[l]), wbf(w_ffn_up[l])], [], [], BF16, "ffn_swiglu")
        xf = _mm_call(_mm_residual_kernel, [gu], [wbf(w_ffn_down[l])], [(xf, 0)], [], F32, "ffn_down")
    out = _rmsnorm(xf, norm_final, F32)
    return out.reshape(batch, seq, d).astype(x.dtype)
```

```python
import functools
import math

import jax
import jax.numpy as jnp
import numpy as np
from jax import lax
from jax.experimental import pallas as pl
from jax.experimental.pallas import tpu as pltpu

F32 = jnp.float32
BF16 = jnp.bfloat16

N_HEADS = 32
HEAD_DIM = 128
N_KV_GROUPS = 4
HEADS_PER_GROUP = N_HEADS // N_KV_GROUPS
CMP_BLOCK = 32
CMP_STRIDE = 16
SEL_BLOCK = 64
LOG2_SEL_BLOCK = 6
SEL_TOPK = 16
WINDOW = 512
CONV_WIDTH = 31
NUM_BUCKETS = 32
MAX_EXACT = NUM_BUCKETS // 2
MAX_DISTANCE = 128
EPS = 1e-6
NEG_INF = -1e30

LANES = 128
SUBLANES = 8
VMEM_LIMIT_BYTES = 56 * 1024 * 1024
MM_VMEM_BUDGET_BYTES = 50 * 1024 * 1024
MM_STEP_OVERHEAD_S = 0.5e-6
MM_PANEL_FETCH_BYTES_PER_S = 3.0e12
LOG2_E = math.log2(math.e)

Q_TILE = 128
FAR_CHUNK = 512
NEAR_KEYS = 2 * Q_TILE
WIN_KEYS = WINDOW + Q_TILE
CMP_HEADS = 8
FAR_HEADS = 1
NEAR_HEADS = 8
WIN_HEADS = 4
CONV_HALO = 32
CONV_LANE_CHUNK = 512
CONV_ROW_BLOCK = 32


def _params(*sem):
    return pltpu.CompilerParams(dimension_semantics=sem, vmem_limit_bytes=VMEM_LIMIT_BYTES)


def _rmsnorm_kernel(x_ref, g_ref, o_ref):
    x = x_ref[...]
    ms = jnp.mean(x * x, axis=-1, keepdims=True)
    o_ref[...] = (x * lax.rsqrt(ms + EPS) * g_ref[...]).astype(o_ref.dtype)


def _rmsnorm(x, g, out_dtype, tm=256):
    m, d = x.shape
    return pl.pallas_call(
        _rmsnorm_kernel,
        grid=(m // tm,),
        in_specs=[pl.BlockSpec((tm, d), lambda i: (i, 0)),
                  pl.BlockSpec((1, d), lambda i: (0, 0))],
        out_specs=pl.BlockSpec((tm, d), lambda i: (i, 0)),
        out_shape=jax.ShapeDtypeStruct((m, d), out_dtype),
        compiler_params=_params("parallel"),
        name="rmsnorm",
    )(x, g.reshape(1, d).astype(F32))


def _dot(a, b):
    return jnp.dot(a, b, preferred_element_type=F32)


def _mm_scale_kernel(a_ref, b_ref, o_ref, *, scale):
    acc = _dot(a_ref[...], b_ref[...])
    if scale != 1.0:
        acc = acc * scale
    o_ref[...] = acc.astype(o_ref.dtype)


def _mm_glu_kernel(a_ref, b1_ref, b2_ref, v1_ref, v2_ref, o_ref):
    a = a_ref[...]
    u = _dot(a, b1_ref[...]) + v1_ref[...]
    g = _dot(a, b2_ref[...]) + v2_ref[...]
    o_ref[...] = (u * jax.nn.sigmoid(g)).astype(o_ref.dtype)


def _mm_swiglu_kernel(a_ref, b1_ref, b2_ref, o_ref):
    a = a_ref[...]
    g = _dot(a, b1_ref[...])
    u = _dot(a, b2_ref[...])
    o_ref[...] = (jax.nn.silu(g) * u).astype(o_ref.dtype)


def _mm_merge_kernel(a1_ref, a2_ref, b1_ref, b2_ref, ga_ref, gb_ref, v_ref, o_ref):
    ya = _dot(a1_ref[...], b1_ref[...])
    yb = _dot(a2_ref[...], b2_ref[...]) + v_ref[...]
    o_ref[...] = (jax.nn.sigmoid(ga_ref[...]) * ya + jax.nn.sigmoid(gb_ref[...]) * yb).astype(o_ref.dtype)


def _mm_residual_kernel(a_ref, b_ref, r_ref, o_ref):
    o_ref[...] = r_ref[...] + _dot(a_ref[...], b_ref[...])


def _mm_tiles(m, n, a_ks, b_ks, n_mn, out_itemsize, col_starts, side_shapes):
    bf16_bytes, f32_bytes, double = 2, 4, 2
    best = None
    for tm in (2048, 1024, 512, 256, 128):
        if m % tm:
            continue
        for tn in (1024, 512, 256, 128):
            if n % tn or any(c % tn for c in col_starts):
                continue
            steps = (m // tm) * (n // tn)
            if any(r % (steps * 2 * SUBLANES) for r, _ in side_shapes):
                continue
            fixed = (sum(k * tn * bf16_bytes * double for k in b_ks)
                     + n_mn * tm * tn * f32_bytes * double
                     + tm * tn * out_itemsize * double
                     + (len(b_ks) + 1) * tm * tn * f32_bytes
                     + sum(r // steps * c * (f32_bytes + bf16_bytes) * double for r, c in side_shapes))
            a_once = sum(tm * k * bf16_bytes for k in a_ks)
            for a_bufs in (2, 1):
                if fixed + a_bufs * a_once <= MM_VMEM_BUDGET_BYTES:
                    cost = (m // tm) * (n // tn) * MM_STEP_OVERHEAD_S
                    if a_bufs == 1:
                        cost += (m // tm) * a_once / MM_PANEL_FETCH_BYTES_PER_S
                    key = (cost, -tm)
                    if best is None or key < best[0]:
                        best = (key, tm, tn, a_bufs)
                    break
    assert best is not None, (m, n, a_ks, b_ks)
    return best[1:]


def _panel(w, row_block=0, col0=0):
    return (w, row_block, col0)


def _with_side_casts(body, n_in, n_side):
    def wrapped(*refs):
        body(*refs[:n_in], refs[n_in + n_side])
        for src, dst in zip(refs[n_in:n_in + n_side], refs[n_in + n_side + 1:]):
            dst[...] = src[...].astype(dst.dtype)
    return wrapped


def _mm_call(body, a_ops, b_ops, mn_ops, vec_ops, out_dtype, name, n=None, side_casts=()):
    m = a_ops[0].shape[0]
    n = b_ops[0][0].shape[1] if n is None else n
    a_ks = [a.shape[1] for a in a_ops]
    b_ks = [a_ks[min(idx, len(a_ks) - 1)] for idx in range(len(b_ops))]
    col_starts = [c for _, _, c in b_ops] + [c for _, c in mn_ops]
    tm, tn, a_bufs = _mm_tiles(m, n, a_ks, b_ks, len(mn_ops), jnp.dtype(out_dtype).itemsize, col_starts,
                               [s.shape for s in side_casts])
    steps_n = n // tn
    a_mode = {} if a_bufs == 2 else dict(pipeline_mode=pl.Buffered(1))
    in_specs = []
    for a in a_ops:
        in_specs.append(pl.BlockSpec((tm, a.shape[1]), lambda i, j: (i, 0), **a_mode))
    for (w, row_block, col0), k in zip(b_ops, b_ks):
        assert col0 % tn == 0 and w.shape[0] % k == 0 and col0 + n <= w.shape[1]
        in_specs.append(pl.BlockSpec((k, tn), lambda i, j, rb=row_block, off=col0 // tn: (rb, j + off)))
    b_ops = [w for w, _, _ in b_ops]
    for _, col0 in mn_ops:
        assert col0 % tn == 0
        in_specs.append(pl.BlockSpec((tm, tn), lambda i, j, off=col0 // tn: (i, j + off)))
    mn_ops = [a for a, _ in mn_ops]
    for _ in vec_ops:
        in_specs.append(pl.BlockSpec((1, tn), lambda i, j: (0, j)))
    out_specs = [pl.BlockSpec((tm, tn), lambda i, j: (i, j))]
    out_shape = [jax.ShapeDtypeStruct((m, n), out_dtype)]
    steps = (m // tm) * steps_n
    for s in side_casts:
        rows, cols = s.shape
        slab = rows // steps
        assert slab * steps == rows and slab % (2 * SUBLANES) == 0, (name, s.shape, steps)
        spec = pl.BlockSpec((slab, cols), lambda i, j: (i * steps_n + j, 0))
        in_specs.append(spec)
        out_specs.append(spec)
        out_shape.append(jax.ShapeDtypeStruct((rows, cols), BF16))
    n_in = len(in_specs) - len(side_casts)
    if side_casts:
        body = _with_side_casts(body, n_in, len(side_casts))
    res = pl.pallas_call(
        body,
        grid=(m // tm, steps_n),
        in_specs=in_specs,
        out_specs=out_specs,
        out_shape=out_shape,
        compiler_params=_params("arbitrary" if side_casts else "parallel", "arbitrary"),
        name=name,
    )(*a_ops, *b_ops, *mn_ops, *vec_ops, *side_casts)
    return tuple(res) if side_casts else res[0]


def _gelu_tanh(x):
    c = math.sqrt(2.0 / math.pi)
    return x * (0.5 * (1.0 + jnp.tanh(c * (x + 0.044715 * (x * x * x)))))


def _compress_kernel(x_ref, pos_ref, w1_ref, w2_ref, o_ref):
    x = x_ref[0, 0]
    p1 = _dot((x + pos_ref[0, 0]).astype(BF16), w1_ref[0, 0])
    p2 = _dot((x + pos_ref[0, 1]).astype(BF16), w1_ref[0, 1])
    ns = x.shape[0]
    pre = p1 + pltpu.roll(p2, ns - 1, 0)
    h = _gelu_tanh(pre).astype(BF16)
    o_ref[0, 0] = _dot(h, w2_ref[0]).astype(o_ref.dtype)


def _compress(xkv, pos, w1, w2):
    _, bg, ns, kw = xkv.shape
    dh = w2.shape[-1]
    return pl.pallas_call(
        _compress_kernel,
        grid=(2, bg),
        in_specs=[pl.BlockSpec((1, 1, ns, kw), lambda k, i: (k, i, 0, 0)),
                  pl.BlockSpec((1, 2, 1, kw), lambda k, i: (k, 0, 0, 0)),
                  pl.BlockSpec((1, 2, kw, dh), lambda k, i: (k, 0, 0, 0)),
                  pl.BlockSpec((1, dh, dh), lambda k, i: (k, 0, 0))],
        out_specs=pl.BlockSpec((1, 1, ns, dh), lambda k, i: (k, i, 0, 0)),
        out_shape=jax.ShapeDtypeStruct((2, bg, ns, dh), BF16),
        compiler_params=_params("parallel", "parallel"),
        name="nsa_compress",
    )(xkv, pos, w1, w2)


def _nt_dot(a, b):
    return lax.dot_general(a, b, (((1,), (1,)), ((), ())), preferred_element_type=F32)


def _nsa_kernel(q_ref, kslc_ref, vslc_ref, kwin_ref, vwin_ref, kcmp_ref, vcmp_ref, gates_ref,
                bnear_ref, bcmp_ref, emat_ref, selmap_t_ref, o_ref,
                m_ref, acc_ref):
    i = pl.program_id(2)
    tq = Q_TILE
    nh = HEADS_PER_GROUP
    dh = HEAD_DIM
    rows = nh * tq
    n_cmp = kcmp_ref.shape[2]

    def q_block(r0, hb):
        return jnp.concatenate([q_ref[0, :, r * dh:(r + 1) * dh] for r in range(r0, r0 + hb)], axis=0)

    def stack_heads(ref, r0, hb):
        return ref[r0:r0 + hb].reshape(hb * tq, ref.shape[-1])

    def block_rows(r0, hb):
        return slice(r0 * tq, (r0 + hb) * tq)

    def add_shared(s, shared):
        hb, n = s.shape[0] // tq, s.shape[1]
        return (s.reshape(hb, tq, n) + shared[None]).reshape(hb * tq, n)

    lane = lax.broadcasted_iota(jnp.int32, (tq, LANES), 1)

    nrow = lax.broadcasted_iota(jnp.int32, (n_cmp, LANES), 0)
    ncol = lax.broadcasted_iota(jnp.int32, (n_cmp, LANES), 1)
    place = jnp.where((nrow == (ncol & 15) + (8 * i - 8)) & (ncol < 48), 1.0, 0.0).astype(BF16)
    k_aug = jnp.concatenate([kcmp_ref[0, 0], place], axis=1)
    tc = lax.broadcasted_iota(jnp.int32, (tq, n_cmp), 0) + i * tq
    nc = lax.broadcasted_iota(jnp.int32, (tq, n_cmp), 1)
    ok_c = (tc - (nc * CMP_STRIDE + CMP_BLOCK - 1)) >= 0
    mask_c = jnp.where(ok_c, 0.0, NEG_INF)
    v_cmp = vcmp_ref[0, 0]
    o_cmp = []
    psum = jnp.zeros((tq, n_cmp), F32)
    for r0 in range(0, nh, CMP_HEADS):
        q_aug = jnp.concatenate([q_block(r0, CMP_HEADS), stack_heads(bcmp_ref, r0, CMP_HEADS)], axis=1)
        s = add_shared(_nt_dot(q_aug, k_aug), mask_c)
        e = jnp.exp2(s - jnp.max(s, axis=-1, keepdims=True))
        p = e / jnp.sum(e, axis=-1, keepdims=True)
        p3 = jnp.where(ok_c[None], p.reshape(CMP_HEADS, tq, n_cmp), 0.0)
        o_cmp.append(_dot(p3.reshape(CMP_HEADS * tq, n_cmp).astype(BF16), v_cmp))
        psum = psum + jnp.sum(p3, axis=0)
    o_cmp = jnp.concatenate(o_cmp, axis=0)

    p_hi = psum.astype(BF16)
    p_lo = (psum - p_hi.astype(F32)).astype(BF16)
    imp_t = _nt_dot(selmap_t_ref[...], p_hi) + _nt_dot(selmap_t_ref[...], p_lo)
    n_blocks = emat_ref.shape[1] // SEL_BLOCK
    assert n_blocks <= LANES and n_blocks % SUBLANES == 0
    blk = lax.broadcasted_iota(jnp.int32, (n_blocks, tq), 0)
    cur_t = (lax.broadcasted_iota(jnp.int32, (n_blocks, tq), 1) + i * tq) >> LOG2_SEL_BLOCK
    valid_t = blk <= cur_t
    forced_t = (blk == 0) | (blk == cur_t) | (blk == cur_t - 1)
    big = 1e30
    x = jnp.where(valid_t, jnp.where(forced_t, big, imp_t[:n_blocks]), -big)
    bcast = [jnp.broadcast_to(x[j:j + 1], (SUBLANES, tq)) for j in range(n_blocks)]
    cnt_groups = []
    for g0 in range(0, n_blocks, SUBLANES):
        xg = x[g0:g0 + SUBLANES]
        blk_g = blk[g0:g0 + SUBLANES]
        cnt = jnp.zeros((SUBLANES, tq), F32)
        for j in range(n_blocks):
            if j < g0:
                beats = bcast[j] >= xg
            elif j >= g0 + SUBLANES:
                beats = bcast[j] > xg
            else:
                beats = (bcast[j] > xg) | ((bcast[j] == xg) & (blk_g > j))
            cnt = cnt + jnp.where(beats, 1.0, 0.0)
        cnt_groups.append(cnt)
    cnt = jnp.concatenate(cnt_groups, axis=0)
    sel_t = jnp.where(valid_t & (cnt < SEL_TOPK), 1.0, 0.0)
    sel_t = jnp.concatenate([sel_t, jnp.zeros((LANES - n_blocks, tq), F32)], axis=0)
    sel = sel_t.T

    def lane_tile(x, n):
        return x if n == 1 else jnp.concatenate([x] * n, axis=1)

    def with_ones(v):
        return jnp.concatenate([v, jnp.ones_like(v)], axis=1)

    m_ref[...] = jnp.full(m_ref.shape, NEG_INF, F32)
    acc_ref[...] = jnp.zeros(acc_ref.shape, F32)

    def online_update(sl, s, v_ones):
        m_prev = m_ref[sl]
        m_next = jnp.maximum(m_prev, jnp.max(s, axis=-1, keepdims=True))
        alpha = jnp.exp2(m_prev - m_next)
        p = jnp.exp2(s - lane_tile(m_next, s.shape[-1] // LANES))
        acc_ref[sl] = lane_tile(alpha, 2) * acc_ref[sl] + _dot(p.astype(BF16), v_ones)
        m_ref[sl] = m_next

    sel_far = jnp.where(lane <= 2 * i - 3, sel, 0.0).astype(BF16)

    def far_body(c, carry):
        c0 = pl.multiple_of(c * FAR_CHUNK, FAR_CHUNK)
        k = kslc_ref[0, pl.ds(c0 + Q_TILE, FAR_CHUNK), :]
        v = with_ones(vslc_ref[0, pl.ds(c0 + Q_TILE, FAR_CHUNK), :])
        hit = _dot(sel_far, emat_ref[:, pl.ds(c0, FAR_CHUNK)])
        mask = (hit - 1.0) * (-NEG_INF)
        for h0 in range(0, nh, FAR_HEADS):
            online_update(block_rows(h0, FAR_HEADS), add_shared(_nt_dot(q_block(h0, FAR_HEADS), k), mask), v)
        return carry

    tiles_per_chunk = FAR_CHUNK // Q_TILE
    lax.fori_loop(0, (i + tiles_per_chunk - 2) // tiles_per_chunk, far_body, 0)

    r0 = pl.multiple_of(i * tq, tq)
    k = kslc_ref[0, pl.ds(r0, NEAR_KEYS), :]
    v = with_ones(vslc_ref[0, pl.ds(r0, NEAR_KEYS), :])
    erow = lax.broadcasted_iota(jnp.int32, (LANES, NEAR_KEYS), 0)
    ecol = lax.broadcasted_iota(jnp.int32, (LANES, NEAR_KEYS), 1)
    e_near = jnp.where(erow == 2 * i - 2 + (ecol >> LOG2_SEL_BLOCK), 1.0, 0.0).astype(BF16)
    hit = _dot(sel.astype(BF16), e_near)
    mask = (hit - 1.0) * (-NEG_INF)
    for h0 in range(0, nh, NEAR_HEADS):
        s = add_shared(_nt_dot(q_block(h0, NEAR_HEADS), k), mask) + stack_heads(bnear_ref, h0, NEAR_HEADS)
        online_update(block_rows(h0, NEAR_HEADS), s, v)

    k_w = kwin_ref[0, pl.ds(r0, WIN_KEYS), :]
    v_w = with_ones(vwin_ref[0, pl.ds(r0, WIN_KEYS), :])
    wrow = lax.broadcasted_iota(jnp.int32, (tq, WINDOW), 0)
    wcol = lax.broadcasted_iota(jnp.int32, (tq, WINDOW), 1)
    ok_w = (wcol >= WINDOW - i * tq) & ((wcol >= tq) | (wcol > wrow))
    mask_w = jnp.where(ok_w, 0.0, NEG_INF)
    old = WIN_KEYS - NEAR_KEYS
    gates = jax.nn.sigmoid(gates_ref[0, 0])
    for h0 in range(0, nh, WIN_HEADS):
        s = _nt_dot(q_block(h0, WIN_HEADS), k_w)
        bias = stack_heads(bnear_ref, h0, WIN_HEADS)
        s = jnp.concatenate([add_shared(s[:, :old], mask_w[:, :old]),
                             add_shared(s[:, old:WINDOW], mask_w[:, old:]) + bias[:, :WINDOW - old],
                             s[:, WINDOW:] + bias[:, WINDOW - old:]], axis=1)
        pv = _dot(jnp.exp2(s - jnp.max(s, axis=-1, keepdims=True)).astype(BF16), v_w)
        o_win = pv[:, :dh] / pv[:, dh:]
        acc = acc_ref[block_rows(h0, WIN_HEADS)]
        o_slc = acc[:, :dh] / acc[:, dh:]
        for r in range(h0, h0 + WIN_HEADS):
            sl = block_rows(r - h0, 1)
            o = (gates[:, 3 * r:3 * r + 1] * o_cmp[block_rows(r, 1)]
                 + gates[:, 3 * r + 1:3 * r + 2] * o_slc[sl]
                 + gates[:, 3 * r + 2:3 * r + 3] * o_win[sl])
            o_ref[0, :, r * dh:(r + 1) * dh] = o.astype(o_ref.dtype)


def _nsa_attention(q, kslc, vslc, kwin, vwin, kvcmp, gates, bnear, bcmp, emat, selmap_t):
    b, t, _ = q.shape
    g = N_KV_GROUPS
    nh = HEADS_PER_GROUP
    dh = HEAD_DIM
    tq = Q_TILE
    n_cmp = kvcmp.shape[2]
    tp_slc = kslc.shape[1]
    tp_win = kwin.shape[1]
    rows = nh * tq
    kv_spec = lambda tp, w: pl.BlockSpec((1, tp, w), lambda bi, gi, ti: (bi, 0, gi))
    return pl.pallas_call(
        _nsa_kernel,
        grid=(b, g, t // tq),
        in_specs=[
            pl.BlockSpec((1, tq, nh * dh), lambda bi, gi, ti: (bi, ti, gi)),
            kv_spec(tp_slc, dh), kv_spec(tp_slc, dh), kv_spec(tp_win, dh), kv_spec(tp_win, dh),
            pl.BlockSpec((1, 1, n_cmp, dh), lambda bi, gi, ti: (0, bi * N_KV_GROUPS + gi, 0, 0)),
            pl.BlockSpec((1, 1, n_cmp, dh), lambda bi, gi, ti: (1, bi * N_KV_GROUPS + gi, 0, 0)),
            pl.BlockSpec((1, 1, tq, LANES), lambda bi, gi, ti: (bi, gi, ti, 0)),
            pl.BlockSpec((nh, tq, NEAR_KEYS), lambda bi, gi, ti: (gi, 0, 0)),
            pl.BlockSpec((nh, tq, LANES), lambda bi, gi, ti: (gi, 0, 0)),
            pl.BlockSpec(emat.shape, lambda bi, gi, ti: (0, 0)),
            pl.BlockSpec(selmap_t.shape, lambda bi, gi, ti: (0, 0)),
        ],
        out_specs=pl.BlockSpec((1, tq, nh * dh), lambda bi, gi, ti: (bi, ti, gi)),
        out_shape=jax.ShapeDtypeStruct((b, t, g * nh * dh), BF16),
        scratch_shapes=[pltpu.VMEM((rows, LANES), F32),
                        pltpu.VMEM((rows, 2 * dh), F32)],
        compiler_params=_params("parallel", "parallel", "arbitrary"),
        name="nsa_attention",
    )(q, kslc, vslc, kwin, vwin, kvcmp, kvcmp, gates, bnear, bcmp, emat, selmap_t)


def _t5_bucket_table(max_dist):
    d = np.arange(max_dist)
    nf = np.maximum(d, MAX_EXACT).astype(np.float32)
    large = MAX_EXACT + (np.log(nf / np.float32(MAX_EXACT)) / np.float32(math.log(MAX_DISTANCE / MAX_EXACT))
                         * np.float32(NUM_BUCKETS - MAX_EXACT)).astype(np.int32)
    large = np.minimum(large, NUM_BUCKETS - 1)
    return np.where(d < MAX_EXACT, d, large).astype(np.int32)


def _nsa_constants(rel_bias, seq):
    tq = Q_TILE
    max_d = NEAR_KEYS
    bucket = _t5_bucket_table(max_d)
    far_cmp = tq - (CMP_BLOCK - 1) + CMP_STRIDE
    assert np.all(bucket[min(far_cmp, tq + 1):] == NUM_BUCKETS - 1)
    table = rel_bias.astype(F32).T * LOG2_E
    delta = table[:, bucket] - table[:, NUM_BUCKETS - 1:]

    def toeplitz(d, ok):
        vals = delta[:, np.clip(d, 0, max_d - 1)]
        return jnp.where(jnp.asarray(ok)[None], vals, NEG_INF)

    t = np.arange(tq)[:, None]
    c = np.arange(NEAR_KEYS)[None, :]
    d_near = t + tq - c
    bnear = toeplitz(d_near, d_near >= 0)
    npr = np.arange(16)[None, :]
    d_cmp = t - CMP_STRIDE * npr + (8 * CMP_STRIDE - (CMP_BLOCK - 1))
    bc = jnp.where(jnp.asarray(d_cmp >= 0)[None], delta[:, np.clip(d_cmp, 0, max_d - 1)], 0.0)
    hi = bc.astype(BF16)
    mid = (bc - hi.astype(F32)).astype(BF16)
    lo = (bc - hi.astype(F32) - mid.astype(F32)).astype(BF16)
    bcmp = jnp.concatenate([hi, mid, lo, jnp.zeros((N_HEADS, tq, LANES - 48), BF16)], axis=-1)

    nsb = seq // SEL_BLOCK
    emat = np.zeros((LANES, seq), np.float32)
    emat[np.arange(seq) // SEL_BLOCK, np.arange(seq)] = 1.0
    ns = seq // CMP_STRIDE
    c_start = np.arange(ns) * CMP_STRIDE
    s_start = np.arange(nsb) * SEL_BLOCK
    ov = np.minimum(c_start[:, None] + CMP_BLOCK, s_start[None, :] + SEL_BLOCK) - np.maximum(c_start[:, None], s_start[None, :])
    ov = np.clip(ov, 0, None) // CMP_STRIDE
    selmap_t = np.zeros((LANES, ns), np.float32)
    selmap_t[:nsb, :ns - 1] = ov[:ns - 1].T
    return bnear, bcmp, jnp.asarray(emat, BF16), jnp.asarray(selmap_t, BF16)


def _conv_kernel(cur_ref, prev_ref, w_ref, b_ref, g_ref, beta_ref, o_ref, shift_ref, conv_ref):
    tt, c = cur_ref.shape[1], cur_ref.shape[2]
    first = pl.program_id(1) == 0
    base = CONV_HALO - (CONV_WIDTH - 1)
    groups = CONV_ROW_BLOCK // SUBLANES

    def chunk_body(ci, carry):
        lanes = pl.ds(pl.multiple_of(ci * CONV_LANE_CHUNK, CONV_LANE_CHUNK), CONV_LANE_CHUNK)
        halo = jnp.where(first, 0.0, prev_ref[0, :, lanes])
        xin = jnp.concatenate([halo, cur_ref[0, :, lanes]], axis=0)
        shift_ref[0] = xin
        keep = tt + CONV_HALO - SUBLANES
        for s in range(1, SUBLANES):
            shift_ref[s, 0:keep, :] = xin[s:s + keep]
        bias = b_ref[:, lanes]
        for r0 in range(0, tt, CONV_ROW_BLOCK):
            acc = jnp.zeros((groups, SUBLANES, CONV_LANE_CHUNK), F32) + bias
            for k in range(CONV_WIDTH):
                s = (base + k) % SUBLANES
                a = r0 + base + k - s
                win = shift_ref[s, a:a + CONV_ROW_BLOCK, :].reshape(groups, SUBLANES, CONV_LANE_CHUNK)
                acc = acc + w_ref[k, :, lanes][None] * win
            conv_ref[r0:r0 + CONV_ROW_BLOCK, lanes] = acc.reshape(CONV_ROW_BLOCK, CONV_LANE_CHUNK)
        return carry

    lax.fori_loop(0, c // CONV_LANE_CHUNK, chunk_body, 0)

    acc = conv_ref[...]
    mu = jnp.mean(acc, axis=-1, keepdims=True)
    xc = acc - mu
    var = jnp.mean(xc * xc, axis=-1, keepdims=True)
    y = xc * lax.rsqrt(var + EPS) * g_ref[...] + beta_ref[...]
    o_ref[0] = jax.nn.silu(y).astype(o_ref.dtype)


def _conformer_conv(h, w_dw, b_dw, ln_g, ln_b, tt=256):
    b, t, c = h.shape
    tt = min(tt, t)
    hb = tt // CONV_HALO
    assert c % CONV_LANE_CHUNK == 0 and tt % CONV_ROW_BLOCK == 0 and t % tt == 0
    wrep = jnp.broadcast_to(w_dw.astype(F32)[:, None, :], (CONV_WIDTH, SUBLANES, c))
    vec = lambda a: a.reshape(1, c).astype(F32)
    return pl.pallas_call(
        _conv_kernel,
        grid=(b, t // tt),
        in_specs=[pl.BlockSpec((1, tt, c), lambda bi, ti: (bi, ti, 0)),
                  pl.BlockSpec((1, CONV_HALO, c), lambda bi, ti: (bi, jnp.maximum(ti * hb - 1, 0), 0)),
                  pl.BlockSpec((CONV_WIDTH, SUBLANES, c), lambda bi, ti: (0, 0, 0)),
                  pl.BlockSpec((1, c), lambda bi, ti: (0, 0)),
                  pl.BlockSpec((1, c), lambda bi, ti: (0, 0)),
                  pl.BlockSpec((1, c), lambda bi, ti: (0, 0))],
        out_specs=pl.BlockSpec((1, tt, c), lambda bi, ti: (bi, ti, 0)),
        out_shape=jax.ShapeDtypeStruct((b, t, c), BF16),
        scratch_shapes=[pltpu.VMEM((SUBLANES, CONV_HALO + tt, CONV_LANE_CHUNK), F32),
                        pltpu.VMEM((tt, c), F32)],
        compiler_params=_params("parallel", "arbitrary"),
        name="conformer_conv",
    )(h, h, wrep, vec(b_dw), vec(ln_g), vec(ln_b))


def _nsa(h, w_in_bf, w_in_l, w_attn_out_l, cmp_pos, cmp_w1, cmp_w2, consts, batch, seq):
    dh = HEAD_DIM
    g = N_KV_GROUPS
    attn_w = N_HEADS * dh
    kv_w = g * dh
    m = h.shape[0]
    c0 = attn_w
    plain = functools.partial(_mm_scale_kernel, scale=1.0)
    q, w_attn_bf = _mm_call(functools.partial(_mm_scale_kernel, scale=dh ** -0.5 * LOG2_E), [h], [_panel(w_in_bf)],
                            [], [], BF16, "proj_q", n=attn_w, side_casts=[w_attn_out_l])
    kvc = _mm_call(plain, [h], [_panel(w_in_bf, 0, c0)], [], [], F32, "proj_kv_cmp", n=2 * kv_w)
    kvr = _mm_call(plain, [h], [_panel(w_in_bf, 0, c0 + 2 * kv_w)], [], [], BF16, "proj_kv", n=4 * kv_w)
    n_gates = 3 * N_HEADS
    w_br = jnp.pad(w_in_l[:, -n_gates:], ((0, 0), (0, LANES - n_gates))).astype(BF16)
    br = _mm_call(plain, [h], [_panel(w_br)], [], [], F32, "proj_branch_gates")

    ns = seq // CMP_STRIDE
    xkv = kvc.reshape(batch, ns, CMP_STRIDE, 2, g, dh).transpose(3, 0, 4, 1, 2, 5).reshape(2, batch * g, ns, CMP_STRIDE * dh)
    pos = cmp_pos.astype(F32).reshape(2, 2, 1, CMP_STRIDE * dh)
    w1 = cmp_w1.astype(BF16).reshape(2, 2, CMP_STRIDE * dh, dh)
    kvcmp = _compress(xkv, pos, w1, cmp_w2.astype(BF16))

    kvr = kvr.reshape(batch, seq, 4 * kv_w)
    padt = lambda a, n: jnp.pad(a, ((0, 0), (n, 0), (0, 0)))
    kslc = padt(kvr[:, :, 0:kv_w], Q_TILE)
    vslc = padt(kvr[:, :, kv_w:2 * kv_w], Q_TILE)
    kwin = padt(kvr[:, :, 2 * kv_w:3 * kv_w], WINDOW)
    vwin = padt(kvr[:, :, 3 * kv_w:4 * kv_w], WINDOW)
    per_g = 3 * HEADS_PER_GROUP
    gates = br[:, :n_gates].reshape(batch, seq, g, per_g).transpose(0, 2, 1, 3)
    gates = jnp.pad(gates, ((0, 0), (0, 0), (0, 0), (0, LANES - per_g)))
    o = _nsa_attention(q.reshape(batch, seq, attn_w), kslc, vslc, kwin, vwin, kvcmp, gates, *consts)
    return o.reshape(m, attn_w), w_attn_bf


def kernel(x, w_in, cmp_pos, cmp_w1, cmp_w2, rel_bias, w_attn_out, b_glu, w_dw, b_dw, conv_ln_g, conv_ln_b, w_conv_out, b_conv_out, w_out, norm_mix, norm_ffn, w_ffn_gate, w_ffn_up, w_ffn_down, norm_final):
    batch, seq, d = x.shape
    depth = w_in.shape[0]
    m = batch * seq
    attn_w = N_HEADS * HEAD_DIM
    kv_w = N_KV_GROUPS * HEAD_DIM
    conv_ch = w_dw.shape[-1]
    c_conv = attn_w + 6 * kv_w
    c_gate = c_conv + 2 * conv_ch
    consts = _nsa_constants(rel_bias, seq)
    xf = x.reshape(m, d).astype(F32)
    row = lambda a: a.reshape(1, -1).astype(F32)
    w_in_bf = w_in[0].astype(BF16)
    for l in range(depth):
        h = _rmsnorm(xf, norm_mix[l], BF16)
        attn, w_attn_bf = _nsa(h, w_in_bf, w_in[l], w_attn_out[l], cmp_pos[l], cmp_w1[l], cmp_w2[l], consts, batch, seq)
        glu, w_conv_bf = _mm_call(_mm_glu_kernel, [h], [_panel(w_in_bf, 0, c_conv), _panel(w_in_bf, 0, c_conv + conv_ch)],
                                  [], [row(b_glu[l, :conv_ch]), row(b_glu[l, conv_ch:])], F32, "proj_conv_glu",
                                  n=conv_ch, side_casts=[w_conv_out[l]])
        conv = _conformer_conv(glu.reshape(batch, seq, conv_ch), w_dw[l], b_dw[l], conv_ln_g[l], conv_ln_b[l])
        gate_ab, w_out_bf = _mm_call(functools.partial(_mm_scale_kernel, scale=1.0), [h], [_panel(w_in_bf, 0, c_gate)],
                                     [], [], F32, "proj_merge_gates", n=2 * d, side_casts=[w_out[l]])
        merged, w_gate_bf = _mm_call(_mm_merge_kernel, [attn, conv.reshape(m, conv_ch)],
                                     [_panel(w_attn_bf), _panel(w_conv_bf)],
                                     [(gate_ab, 0), (gate_ab, d)], [row(b_conv_out[l])], BF16, "merge",
                                     side_casts=[w_ffn_gate[l]])
        xf, w_up_bf = _mm_call(_mm_residual_kernel, [merged], [_panel(w_out_bf)], [(xf, 0)], [], F32, "mix_out",
                               side_casts=[w_ffn_up[l]])
        h = _rmsnorm(xf, norm_ffn[l], BF16)
        gu, w_down_bf = _mm_call(_mm_swiglu_kernel, [h], [_panel(w_gate_bf), _panel(w_up_bf)], [], [], BF16,
                                 "ffn_swiglu", side_casts=[w_ffn_down[l]])
        nxt = [w_in[l + 1]] if l + 1 < depth else []
        res = _mm_call(_mm_residual_kernel, [gu], [_panel(w_down_bf)], [(xf, 0)], [], F32, "ffn_down", side_casts=nxt)
        xf, w_in_bf = res if nxt else (res, None)
    out = _rmsnorm(xf, norm_final, F32)
    return out.reshape(batch, seq, d).astype(x.dtype)
```

```python
import functools
import math

import jax
import jax.numpy as jnp
import numpy as np
from jax import lax
from jax.experimental import pallas as pl
from jax.experimental.pallas import tpu as pltpu

F32 = jnp.float32
BF16 = jnp.bfloat16

N_HEADS = 32
HEAD_DIM = 128
N_KV_GROUPS = 4
HEADS_PER_GROUP = N_HEADS // N_KV_GROUPS
CMP_BLOCK = 32
CMP_STRIDE = 16
SEL_BLOCK = 64
LOG2_SEL_BLOCK = 6
SEL_TOPK = 16
WINDOW = 512
CONV_WIDTH = 31
NUM_BUCKETS = 32
MAX_EXACT = NUM_BUCKETS // 2
MAX_DISTANCE = 128
EPS = 1e-6
NEG_INF = -1e30

LANES = 128
SUBLANES = 8
VMEM_LIMIT_BYTES = 56 * 1024 * 1024
MM_VMEM_BUDGET_BYTES = 50 * 1024 * 1024
MM_STEP_OVERHEAD_S = 0.5e-6
MM_PANEL_FETCH_BYTES_PER_S = 3.0e12
LOG2_E = math.log2(math.e)

Q_TILE = 128
FAR_CHUNK = 512
NEAR_KEYS = 2 * Q_TILE
WIN_KEYS = WINDOW + Q_TILE
CMP_HEADS = 8
FAR_HEADS = 1
NEAR_HEADS = 8
WIN_HEADS = 4
CONV_HALO = 32
CONV_LANE_CHUNK = 512
CONV_ROW_BLOCK = 32


def _params(*sem):
    return pltpu.CompilerParams(dimension_semantics=sem, vmem_limit_bytes=VMEM_LIMIT_BYTES)


def _rmsnorm_kernel(x_ref, g_ref, o_ref):
    x = x_ref[...]
    ms = jnp.mean(x * x, axis=-1, keepdims=True)
    o_ref[...] = (x * lax.rsqrt(ms + EPS) * g_ref[...]).astype(o_ref.dtype)


def _rmsnorm(x, g, out_dtype, tm=256):
    m, d = x.shape
    return pl.pallas_call(
        _rmsnorm_kernel,
        grid=(m // tm,),
        in_specs=[pl.BlockSpec((tm, d), lambda i: (i, 0)),
                  pl.BlockSpec((1, d), lambda i: (0, 0))],
        out_specs=pl.BlockSpec((tm, d), lambda i: (i, 0)),
        out_shape=jax.ShapeDtypeStruct((m, d), out_dtype),
        compiler_params=_params("parallel"),
        name="rmsnorm",
    )(x, g.reshape(1, d).astype(F32))


def _dot(a, b):
    return jnp.dot(a, b, preferred_element_type=F32)


def _mm_scale_kernel(a_ref, b_ref, o_ref, *, scale):
    acc = _dot(a_ref[...], b_ref[...])
    if scale != 1.0:
        acc = acc * scale
    o_ref[...] = acc.astype(o_ref.dtype)


def _mm_glu_kernel(a_ref, b1_ref, b2_ref, v1_ref, v2_ref, o_ref):
    a = a_ref[...]
    u = _dot(a, b1_ref[...]) + v1_ref[...]
    g = _dot(a, b2_ref[...]) + v2_ref[...]
    o_ref[...] = (u * jax.nn.sigmoid(g)).astype(o_ref.dtype)


def _mm_swiglu_kernel(a_ref, b1_ref, b2_ref, o_ref):
    a = a_ref[...]
    g = _dot(a, b1_ref[...])
    u = _dot(a, b2_ref[...])
    o_ref[...] = (jax.nn.silu(g) * u).astype(o_ref.dtype)


def _mm_merge_kernel(a1_ref, a2_ref, b1_ref, b2_ref, ga_ref, gb_ref, v_ref, o_ref):
    ya = _dot(a1_ref[...], b1_ref[...])
    yb = _dot(a2_ref[...], b2_ref[...]) + v_ref[...]
    o_ref[...] = (jax.nn.sigmoid(ga_ref[...]) * ya + jax.nn.sigmoid(gb_ref[...]) * yb).astype(o_ref.dtype)


def _mm_residual_kernel(a_ref, b_ref, r_ref, o_ref):
    o_ref[...] = r_ref[...] + _dot(a_ref[...], b_ref[...])


def _mm_tiles(m, n, a_ks, b_ks, n_mn, out_itemsize, col_starts, side_shapes):
    bf16_bytes, f32_bytes, double = 2, 4, 2
    best = None
    for tm in (2048, 1024, 512, 256, 128):
        if m % tm:
            continue
        for tn in (1024, 512, 256, 128):
            if n % tn or any(c % tn for c in col_starts):
                continue
            steps = (m // tm) * (n // tn)
            if any(r % (steps * 2 * SUBLANES) for r, _ in side_shapes):
                continue
            fixed = (sum(k * tn * bf16_bytes * double for k in b_ks)
                     + n_mn * tm * tn * f32_bytes * double
                     + tm * tn * out_itemsize * double
                     + (len(b_ks) + 1) * tm * tn * f32_bytes
                     + sum(r // steps * c * (f32_bytes + bf16_bytes) * double for r, c in side_shapes))
            a_once = sum(tm * k * bf16_bytes for k in a_ks)
            for a_bufs in (2, 1):
                if fixed + a_bufs * a_once <= MM_VMEM_BUDGET_BYTES:
                    cost = (m // tm) * (n // tn) * MM_STEP_OVERHEAD_S
                    if a_bufs == 1:
                        cost += (m // tm) * a_once / MM_PANEL_FETCH_BYTES_PER_S
                    key = (cost, -tm)
                    if best is None or key < best[0]:
                        best = (key, tm, tn, a_bufs)
                    break
    assert best is not None, (m, n, a_ks, b_ks)
    return best[1:]


def _panel(w, row_block=0, col0=0):
    return (w, row_block, col0)


def _with_side_casts(body, n_in, n_side):
    def wrapped(*refs):
        body(*refs[:n_in], refs[n_in + n_side])
        for src, dst in zip(refs[n_in:n_in + n_side], refs[n_in + n_side + 1:]):
            dst[...] = src[...].astype(dst.dtype)
    return wrapped


def _mm_call(body, a_ops, b_ops, mn_ops, vec_ops, out_dtype, name, n=None, side_casts=()):
    m = a_ops[0].shape[0]
    n = b_ops[0][0].shape[1] if n is None else n
    a_ks = [a.shape[1] for a in a_ops]
    b_ks = [a_ks[min(idx, len(a_ks) - 1)] for idx in range(len(b_ops))]
    col_starts = [c for _, _, c in b_ops] + [c for _, c in mn_ops]
    tm, tn, a_bufs = _mm_tiles(m, n, a_ks, b_ks, len(mn_ops), jnp.dtype(out_dtype).itemsize, col_starts,
                               [s.shape[1:] for s, _ in side_casts])
    steps_n = n // tn
    a_mode = {} if a_bufs == 2 else dict(pipeline_mode=pl.Buffered(1))
    in_specs = []
    for a in a_ops:
        in_specs.append(pl.BlockSpec((tm, a.shape[1]), lambda i, j: (i, 0), **a_mode))
    for (w, row_block, col0), k in zip(b_ops, b_ks):
        assert col0 % tn == 0 and w.shape[0] % k == 0 and col0 + n <= w.shape[1]
        in_specs.append(pl.BlockSpec((k, tn), lambda i, j, rb=row_block, off=col0 // tn: (rb, j + off)))
    b_ops = [w for w, _, _ in b_ops]
    for _, col0 in mn_ops:
        assert col0 % tn == 0
        in_specs.append(pl.BlockSpec((tm, tn), lambda i, j, off=col0 // tn: (i, j + off)))
    mn_ops = [a for a, _ in mn_ops]
    for _ in vec_ops:
        in_specs.append(pl.BlockSpec((1, tn), lambda i, j: (0, j)))
    out_specs = [pl.BlockSpec((tm, tn), lambda i, j: (i, j))]
    out_shape = [jax.ShapeDtypeStruct((m, n), out_dtype)]
    steps = (m // tm) * steps_n
    for s, layer in side_casts:
        _, rows, cols = s.shape
        slab = rows // steps
        assert slab * steps == rows and slab % (2 * SUBLANES) == 0, (name, s.shape, steps)
        in_specs.append(pl.BlockSpec((None, slab, cols), lambda i, j, l=layer: (l, i * steps_n + j, 0)))
        out_specs.append(pl.BlockSpec((slab, cols), lambda i, j: (i * steps_n + j, 0)))
        out_shape.append(jax.ShapeDtypeStruct((rows, cols), BF16))
    n_in = len(in_specs) - len(side_casts)
    if side_casts:
        body = _with_side_casts(body, n_in, len(side_casts))
    res = pl.pallas_call(
        body,
        grid=(m // tm, steps_n),
        in_specs=in_specs,
        out_specs=out_specs,
        out_shape=out_shape,
        compiler_params=_params("arbitrary" if side_casts else "parallel", "arbitrary"),
        name=name,
    )(*a_ops, *b_ops, *mn_ops, *vec_ops, *[s for s, _ in side_casts])
    return tuple(res) if side_casts else res[0]


def _gelu_tanh(x):
    c = math.sqrt(2.0 / math.pi)
    return x * (0.5 * (1.0 + jnp.tanh(c * (x + 0.044715 * (x * x * x)))))


def _compress_kernel(x_ref, pos_ref, w1_ref, w2_ref, o_ref):
    x = x_ref[0, 0]
    p1 = _dot((x + pos_ref[0, 0]).astype(BF16), w1_ref[0, 0])
    p2 = _dot((x + pos_ref[0, 1]).astype(BF16), w1_ref[0, 1])
    ns = x.shape[0]
    pre = p1 + pltpu.roll(p2, ns - 1, 0)
    h = _gelu_tanh(pre).astype(BF16)
    o_ref[0, 0] = _dot(h, w2_ref[0]).astype(o_ref.dtype)


def _compress(xkv, pos, w1, w2):
    _, bg, ns, kw = xkv.shape
    dh = w2.shape[-1]
    return pl.pallas_call(
        _compress_kernel,
        grid=(2, bg),
        in_specs=[pl.BlockSpec((1, 1, ns, kw), lambda k, i: (k, i, 0, 0)),
                  pl.BlockSpec((1, 2, 1, kw), lambda k, i: (k, 0, 0, 0)),
                  pl.BlockSpec((1, 2, kw, dh), lambda k, i: (k, 0, 0, 0)),
                  pl.BlockSpec((1, dh, dh), lambda k, i: (k, 0, 0))],
        out_specs=pl.BlockSpec((1, 1, ns, dh), lambda k, i: (k, i, 0, 0)),
        out_shape=jax.ShapeDtypeStruct((2, bg, ns, dh), BF16),
        compiler_params=_params("parallel", "parallel"),
        name="nsa_compress",
    )(xkv, pos, w1, w2)


def _nt_dot(a, b):
    return lax.dot_general(a, b, (((1,), (1,)), ((), ())), preferred_element_type=F32)


def _nsa_kernel(q_ref, kslc_ref, vslc_ref, kwin_ref, vwin_ref, kcmp_ref, vcmp_ref, gates_ref,
                bnear_ref, bcmp_ref, emat_ref, selmap_t_ref, o_ref,
                m_ref, acc_ref):
    i = pl.program_id(2)
    tq = Q_TILE
    nh = HEADS_PER_GROUP
    dh = HEAD_DIM
    rows = nh * tq
    n_cmp = kcmp_ref.shape[2]

    def q_block(r0, hb):
        return jnp.concatenate([q_ref[0, :, r * dh:(r + 1) * dh] for r in range(r0, r0 + hb)], axis=0)

    def stack_heads(ref, r0, hb):
        return ref[r0:r0 + hb].reshape(hb * tq, ref.shape[-1])

    def block_rows(r0, hb):
        return slice(r0 * tq, (r0 + hb) * tq)

    def add_shared(s, shared):
        hb, n = s.shape[0] // tq, s.shape[1]
        return (s.reshape(hb, tq, n) + shared[None]).reshape(hb * tq, n)

    lane = lax.broadcasted_iota(jnp.int32, (tq, LANES), 1)

    nrow = lax.broadcasted_iota(jnp.int32, (n_cmp, LANES), 0)
    ncol = lax.broadcasted_iota(jnp.int32, (n_cmp, LANES), 1)
    place = jnp.where((nrow == (ncol & 15) + (8 * i - 8)) & (ncol < 48), 1.0, 0.0).astype(BF16)
    k_aug = jnp.concatenate([kcmp_ref[0, 0], place], axis=1)
    tc = lax.broadcasted_iota(jnp.int32, (tq, n_cmp), 0) + i * tq
    nc = lax.broadcasted_iota(jnp.int32, (tq, n_cmp), 1)
    ok_c = (tc - (nc * CMP_STRIDE + CMP_BLOCK - 1)) >= 0
    mask_c = jnp.where(ok_c, 0.0, NEG_INF)
    v_cmp = vcmp_ref[0, 0]
    o_cmp = []
    psum = jnp.zeros((tq, n_cmp), F32)
    for r0 in range(0, nh, CMP_HEADS):
        q_aug = jnp.concatenate([q_block(r0, CMP_HEADS), stack_heads(bcmp_ref, r0, CMP_HEADS)], axis=1)
        s = add_shared(_nt_dot(q_aug, k_aug), mask_c)
        e = jnp.exp2(s - jnp.max(s, axis=-1, keepdims=True))
        p = e / jnp.sum(e, axis=-1, keepdims=True)
        p3 = jnp.where(ok_c[None], p.reshape(CMP_HEADS, tq, n_cmp), 0.0)
        o_cmp.append(_dot(p3.reshape(CMP_HEADS * tq, n_cmp).astype(BF16), v_cmp))
        psum = psum + jnp.sum(p3, axis=0)
    o_cmp = jnp.concatenate(o_cmp, axis=0)

    p_hi = psum.astype(BF16)
    p_lo = (psum - p_hi.astype(F32)).astype(BF16)
    imp_t = _nt_dot(selmap_t_ref[...], p_hi) + _nt_dot(selmap_t_ref[...], p_lo)
    n_blocks = emat_ref.shape[1] // SEL_BLOCK
    assert n_blocks <= LANES and n_blocks % SUBLANES == 0
    blk = lax.broadcasted_iota(jnp.int32, (n_blocks, tq), 0)
    cur_t = (lax.broadcasted_iota(jnp.int32, (n_blocks, tq), 1) + i * tq) >> LOG2_SEL_BLOCK
    valid_t = blk <= cur_t
    forced_t = (blk == 0) | (blk == cur_t) | (blk == cur_t - 1)
    big = 1e30
    x = jnp.where(valid_t, jnp.where(forced_t, big, imp_t[:n_blocks]), -big)
    bcast = [jnp.broadcast_to(x[j:j + 1], (SUBLANES, tq)) for j in range(n_blocks)]
    cnt_groups = []
    for g0 in range(0, n_blocks, SUBLANES):
        xg = x[g0:g0 + SUBLANES]
        blk_g = blk[g0:g0 + SUBLANES]
        cnt = jnp.zeros((SUBLANES, tq), F32)
        for j in range(n_blocks):
            if j < g0:
                beats = bcast[j] >= xg
            elif j >= g0 + SUBLANES:
                beats = bcast[j] > xg
            else:
                beats = (bcast[j] > xg) | ((bcast[j] == xg) & (blk_g > j))
            cnt = cnt + jnp.where(beats, 1.0, 0.0)
        cnt_groups.append(cnt)
    cnt = jnp.concatenate(cnt_groups, axis=0)
    sel_t = jnp.where(valid_t & (cnt < SEL_TOPK), 1.0, 0.0)
    sel_t = jnp.concatenate([sel_t, jnp.zeros((LANES - n_blocks, tq), F32)], axis=0)
    sel = sel_t.T

    def lane_tile(x, n):
        return x if n == 1 else jnp.concatenate([x] * n, axis=1)

    def with_ones(v):
        return jnp.concatenate([v, jnp.ones_like(v)], axis=1)

    m_ref[...] = jnp.full(m_ref.shape, NEG_INF, F32)
    acc_ref[...] = jnp.zeros(acc_ref.shape, F32)

    def online_update(sl, s, v_ones):
        m_prev = m_ref[sl]
        m_next = jnp.maximum(m_prev, jnp.max(s, axis=-1, keepdims=True))
        alpha = jnp.exp2(m_prev - m_next)
        p = jnp.exp2(s - lane_tile(m_next, s.shape[-1] // LANES))
        acc_ref[sl] = lane_tile(alpha, 2) * acc_ref[sl] + _dot(p.astype(BF16), v_ones)
        m_ref[sl] = m_next

    sel_far = jnp.where(lane <= 2 * i - 3, sel, 0.0).astype(BF16)

    def far_body(c, carry):
        c0 = pl.multiple_of(c * FAR_CHUNK, FAR_CHUNK)
        k = kslc_ref[0, pl.ds(c0 + Q_TILE, FAR_CHUNK), :]
        v = with_ones(vslc_ref[0, pl.ds(c0 + Q_TILE, FAR_CHUNK), :])
        hit = _dot(sel_far, emat_ref[:, pl.ds(c0, FAR_CHUNK)])
        mask = (hit - 1.0) * (-NEG_INF)
        for h0 in range(0, nh, FAR_HEADS):
            online_update(block_rows(h0, FAR_HEADS), add_shared(_nt_dot(q_block(h0, FAR_HEADS), k), mask), v)
        return carry

    tiles_per_chunk = FAR_CHUNK // Q_TILE
    lax.fori_loop(0, (i + tiles_per_chunk - 2) // tiles_per_chunk, far_body, 0)

    r0 = pl.multiple_of(i * tq, tq)
    k = kslc_ref[0, pl.ds(r0, NEAR_KEYS), :]
    v = with_ones(vslc_ref[0, pl.ds(r0, NEAR_KEYS), :])
    erow = lax.broadcasted_iota(jnp.int32, (LANES, NEAR_KEYS), 0)
    ecol = lax.broadcasted_iota(jnp.int32, (LANES, NEAR_KEYS), 1)
    e_near = jnp.where(erow == 2 * i - 2 + (ecol >> LOG2_SEL_BLOCK), 1.0, 0.0).astype(BF16)
    hit = _dot(sel.astype(BF16), e_near)
    mask = (hit - 1.0) * (-NEG_INF)
    for h0 in range(0, nh, NEAR_HEADS):
        s = add_shared(_nt_dot(q_block(h0, NEAR_HEADS), k), mask) + stack_heads(bnear_ref, h0, NEAR_HEADS)
        online_update(block_rows(h0, NEAR_HEADS), s, v)

    k_w = kwin_ref[0, pl.ds(r0, WIN_KEYS), :]
    v_w = with_ones(vwin_ref[0, pl.ds(r0, WIN_KEYS), :])
    wrow = lax.broadcasted_iota(jnp.int32, (tq, WINDOW), 0)
    wcol = lax.broadcasted_iota(jnp.int32, (tq, WINDOW), 1)
    ok_w = (wcol >= WINDOW - i * tq) & ((wcol >= tq) | (wcol > wrow))
    mask_w = jnp.where(ok_w, 0.0, NEG_INF)
    old = WIN_KEYS - NEAR_KEYS
    gates = jax.nn.sigmoid(gates_ref[0, 0])
    for h0 in range(0, nh, WIN_HEADS):
        s = _nt_dot(q_block(h0, WIN_HEADS), k_w)
        bias = stack_heads(bnear_ref, h0, WIN_HEADS)
        s = jnp.concatenate([add_shared(s[:, :old], mask_w[:, :old]),
                             add_shared(s[:, old:WINDOW], mask_w[:, old:]) + bias[:, :WINDOW - old],
                             s[:, WINDOW:] + bias[:, WINDOW - old:]], axis=1)
        pv = _dot(jnp.exp2(s - jnp.max(s, axis=-1, keepdims=True)).astype(BF16), v_w)
        o_win = pv[:, :dh] / pv[:, dh:]
        acc = acc_ref[block_rows(h0, WIN_HEADS)]
        o_slc = acc[:, :dh] / acc[:, dh:]
        for r in range(h0, h0 + WIN_HEADS):
            sl = block_rows(r - h0, 1)
            o = (gates[:, 3 * r:3 * r + 1] * o_cmp[block_rows(r, 1)]
                 + gates[:, 3 * r + 1:3 * r + 2] * o_slc[sl]
                 + gates[:, 3 * r + 2:3 * r + 3] * o_win[sl])
            o_ref[0, :, r * dh:(r + 1) * dh] = o.astype(o_ref.dtype)


def _nsa_attention(q, kslc, vslc, kwin, vwin, kvcmp, gates, bnear, bcmp, emat, selmap_t):
    b, t, _ = q.shape
    g = N_KV_GROUPS
    nh = HEADS_PER_GROUP
    dh = HEAD_DIM
    tq = Q_TILE
    n_cmp = kvcmp.shape[2]
    tp_slc = kslc.shape[1]
    tp_win = kwin.shape[1]
    rows = nh * tq
    kv_spec = lambda tp, w: pl.BlockSpec((1, tp, w), lambda bi, gi, ti: (bi, 0, gi))
    return pl.pallas_call(
        _nsa_kernel,
        grid=(b, g, t // tq),
        in_specs=[
            pl.BlockSpec((1, tq, nh * dh), lambda bi, gi, ti: (bi, ti, gi)),
            kv_spec(tp_slc, dh), kv_spec(tp_slc, dh), kv_spec(tp_win, dh), kv_spec(tp_win, dh),
            pl.BlockSpec((1, 1, n_cmp, dh), lambda bi, gi, ti: (0, bi * N_KV_GROUPS + gi, 0, 0)),
            pl.BlockSpec((1, 1, n_cmp, dh), lambda bi, gi, ti: (1, bi * N_KV_GROUPS + gi, 0, 0)),
            pl.BlockSpec((1, 1, tq, LANES), lambda bi, gi, ti: (bi, gi, ti, 0)),
            pl.BlockSpec((nh, tq, NEAR_KEYS), lambda bi, gi, ti: (gi, 0, 0)),
            pl.BlockSpec((nh, tq, LANES), lambda bi, gi, ti: (gi, 0, 0)),
            pl.BlockSpec(emat.shape, lambda bi, gi, ti: (0, 0)),
            pl.BlockSpec(selmap_t.shape, lambda bi, gi, ti: (0, 0)),
        ],
        out_specs=pl.BlockSpec((1, tq, nh * dh), lambda bi, gi, ti: (bi, ti, gi)),
        out_shape=jax.ShapeDtypeStruct((b, t, g * nh * dh), BF16),
        scratch_shapes=[pltpu.VMEM((rows, LANES), F32),
                        pltpu.VMEM((rows, 2 * dh), F32)],
        compiler_params=_params("parallel", "parallel", "arbitrary"),
        name="nsa_attention",
    )(q, kslc, vslc, kwin, vwin, kvcmp, kvcmp, gates, bnear, bcmp, emat, selmap_t)


def _t5_bucket_table(max_dist):
    d = np.arange(max_dist)
    nf = np.maximum(d, MAX_EXACT).astype(np.float32)
    large = MAX_EXACT + (np.log(nf / np.float32(MAX_EXACT)) / np.float32(math.log(MAX_DISTANCE / MAX_EXACT))
                         * np.float32(NUM_BUCKETS - MAX_EXACT)).astype(np.int32)
    large = np.minimum(large, NUM_BUCKETS - 1)
    return np.where(d < MAX_EXACT, d, large).astype(np.int32)


def _nsa_constants(rel_bias, seq):
    tq = Q_TILE
    max_d = NEAR_KEYS
    bucket = _t5_bucket_table(max_d)
    far_cmp = tq - (CMP_BLOCK - 1) + CMP_STRIDE
    assert np.all(bucket[min(far_cmp, tq + 1):] == NUM_BUCKETS - 1)
    table = rel_bias.astype(F32).T * LOG2_E
    delta = table[:, bucket] - table[:, NUM_BUCKETS - 1:]

    strip = jnp.concatenate([delta[:, ::-1], jnp.full((N_HEADS, tq - 1), NEG_INF, F32)], axis=1)
    bnear = jnp.stack([strip[:, tq - 1 - t:tq - 1 - t + NEAR_KEYS] for t in range(tq)], axis=1)
    t = np.arange(tq)[:, None]
    npr = np.arange(16)[None, :]
    d_cmp = t - CMP_STRIDE * npr + (8 * CMP_STRIDE - (CMP_BLOCK - 1))
    bc = jnp.where(jnp.asarray(d_cmp >= 0)[None], delta[:, np.clip(d_cmp, 0, max_d - 1)], 0.0)
    hi = bc.astype(BF16)
    mid = (bc - hi.astype(F32)).astype(BF16)
    lo = (bc - hi.astype(F32) - mid.astype(F32)).astype(BF16)
    bcmp = jnp.concatenate([hi, mid, lo, jnp.zeros((N_HEADS, tq, LANES - 48), BF16)], axis=-1)

    nsb = seq // SEL_BLOCK
    emat = np.zeros((LANES, seq), np.float32)
    emat[np.arange(seq) // SEL_BLOCK, np.arange(seq)] = 1.0
    ns = seq // CMP_STRIDE
    c_start = np.arange(ns) * CMP_STRIDE
    s_start = np.arange(nsb) * SEL_BLOCK
    ov = np.minimum(c_start[:, None] + CMP_BLOCK, s_start[None, :] + SEL_BLOCK) - np.maximum(c_start[:, None], s_start[None, :])
    ov = np.clip(ov, 0, None) // CMP_STRIDE
    selmap_t = np.zeros((LANES, ns), np.float32)
    selmap_t[:nsb, :ns - 1] = ov[:ns - 1].T
    return bnear, bcmp, jnp.asarray(emat, BF16), jnp.asarray(selmap_t, BF16)


def _conv_kernel(cur_ref, prev_ref, w_ref, b_ref, g_ref, beta_ref, o_ref, shift_ref, conv_ref):
    tt, c = cur_ref.shape[1], cur_ref.shape[2]
    first = pl.program_id(1) == 0
    base = CONV_HALO - (CONV_WIDTH - 1)
    groups = CONV_ROW_BLOCK // SUBLANES

    def chunk_body(ci, carry):
        lanes = pl.ds(pl.multiple_of(ci * CONV_LANE_CHUNK, CONV_LANE_CHUNK), CONV_LANE_CHUNK)
        halo = jnp.where(first, 0.0, prev_ref[0, :, lanes])
        xin = jnp.concatenate([halo, cur_ref[0, :, lanes]], axis=0)
        shift_ref[0] = xin
        keep = tt + CONV_HALO - SUBLANES
        for s in range(1, SUBLANES):
            shift_ref[s, 0:keep, :] = xin[s:s + keep]
        bias = b_ref[:, lanes]
        for r0 in range(0, tt, CONV_ROW_BLOCK):
            acc = jnp.zeros((groups, SUBLANES, CONV_LANE_CHUNK), F32) + bias
            for k in range(CONV_WIDTH):
                s = (base + k) % SUBLANES
                a = r0 + base + k - s
                win = shift_ref[s, a:a + CONV_ROW_BLOCK, :].reshape(groups, SUBLANES, CONV_LANE_CHUNK)
                acc = acc + w_ref[k, :, lanes][None] * win
            conv_ref[r0:r0 + CONV_ROW_BLOCK, lanes] = acc.reshape(CONV_ROW_BLOCK, CONV_LANE_CHUNK)
        return carry

    lax.fori_loop(0, c // CONV_LANE_CHUNK, chunk_body, 0)

    acc = conv_ref[...]
    mu = jnp.mean(acc, axis=-1, keepdims=True)
    xc = acc - mu
    var = jnp.mean(xc * xc, axis=-1, keepdims=True)
    y = xc * lax.rsqrt(var + EPS) * g_ref[...] + beta_ref[...]
    o_ref[0] = jax.nn.silu(y).astype(o_ref.dtype)


def _conformer_conv(h, w_dw, b_dw, ln_g, ln_b, tt=256):
    b, t, c = h.shape
    tt = min(tt, t)
    hb = tt // CONV_HALO
    assert c % CONV_LANE_CHUNK == 0 and tt % CONV_ROW_BLOCK == 0 and t % tt == 0
    wrep = jnp.broadcast_to(w_dw.astype(F32)[:, None, :], (CONV_WIDTH, SUBLANES, c))
    vec = lambda a: a.reshape(1, c).astype(F32)
    return pl.pallas_call(
        _conv_kernel,
        grid=(b, t // tt),
        in_specs=[pl.BlockSpec((1, tt, c), lambda bi, ti: (bi, ti, 0)),
                  pl.BlockSpec((1, CONV_HALO, c), lambda bi, ti: (bi, jnp.maximum(ti * hb - 1, 0), 0)),
                  pl.BlockSpec((CONV_WIDTH, SUBLANES, c), lambda bi, ti: (0, 0, 0)),
                  pl.BlockSpec((1, c), lambda bi, ti: (0, 0)),
                  pl.BlockSpec((1, c), lambda bi, ti: (0, 0)),
                  pl.BlockSpec((1, c), lambda bi, ti: (0, 0))],
        out_specs=pl.BlockSpec((1, tt, c), lambda bi, ti: (bi, ti, 0)),
        out_shape=jax.ShapeDtypeStruct((b, t, c), BF16),
        scratch_shapes=[pltpu.VMEM((SUBLANES, CONV_HALO + tt, CONV_LANE_CHUNK), F32),
                        pltpu.VMEM((tt, c), F32)],
        compiler_params=_params("parallel", "arbitrary"),
        name="conformer_conv",
    )(h, h, wrep, vec(b_dw), vec(ln_g), vec(ln_b))


def _nsa(h, w_in_bf, w_gates_l, w_attn_out_side, cmp_pos, cmp_w1, cmp_w2, consts, batch, seq):
    dh = HEAD_DIM
    g = N_KV_GROUPS
    attn_w = N_HEADS * dh
    kv_w = g * dh
    m = h.shape[0]
    c0 = attn_w
    plain = functools.partial(_mm_scale_kernel, scale=1.0)
    q, w_attn_bf = _mm_call(functools.partial(_mm_scale_kernel, scale=dh ** -0.5 * LOG2_E), [h], [_panel(w_in_bf)],
                            [], [], BF16, "proj_q", n=attn_w, side_casts=[w_attn_out_side])
    kvc = _mm_call(plain, [h], [_panel(w_in_bf, 0, c0)], [], [], F32, "proj_kv_cmp", n=2 * kv_w)
    kvr = _mm_call(plain, [h], [_panel(w_in_bf, 0, c0 + 2 * kv_w)], [], [], BF16, "proj_kv", n=4 * kv_w)
    n_gates = 3 * N_HEADS
    w_br = jnp.pad(w_gates_l, ((0, 0), (0, LANES - n_gates))).astype(BF16)
    br = _mm_call(plain, [h], [_panel(w_br)], [], [], F32, "proj_branch_gates")

    ns = seq // CMP_STRIDE
    xkv = kvc.reshape(batch, ns, CMP_STRIDE, 2, g, dh).transpose(3, 0, 4, 1, 2, 5).reshape(2, batch * g, ns, CMP_STRIDE * dh)
    pos = cmp_pos.astype(F32).reshape(2, 2, 1, CMP_STRIDE * dh)
    w1 = cmp_w1.astype(BF16).reshape(2, 2, CMP_STRIDE * dh, dh)
    kvcmp = _compress(xkv, pos, w1, cmp_w2.astype(BF16))

    kvr = kvr.reshape(batch, seq, 4 * kv_w)
    padt = lambda a, n: jnp.pad(a, ((0, 0), (n, 0), (0, 0)))
    kslc = padt(kvr[:, :, 0:kv_w], Q_TILE)
    vslc = padt(kvr[:, :, kv_w:2 * kv_w], Q_TILE)
    kwin = padt(kvr[:, :, 2 * kv_w:3 * kv_w], WINDOW)
    vwin = padt(kvr[:, :, 3 * kv_w:4 * kv_w], WINDOW)
    per_g = 3 * HEADS_PER_GROUP
    gates = br[:, :n_gates].reshape(batch, seq, g, per_g).transpose(0, 2, 1, 3)
    gates = jnp.pad(gates, ((0, 0), (0, 0), (0, 0), (0, LANES - per_g)))
    o = _nsa_attention(q.reshape(batch, seq, attn_w), kslc, vslc, kwin, vwin, kvcmp, gates, *consts)
    return o.reshape(m, attn_w), w_attn_bf


def kernel(x, w_in, cmp_pos, cmp_w1, cmp_w2, rel_bias, w_attn_out, b_glu, w_dw, b_dw, conv_ln_g, conv_ln_b, w_conv_out, b_conv_out, w_out, norm_mix, norm_ffn, w_ffn_gate, w_ffn_up, w_ffn_down, norm_final):
    batch, seq, d = x.shape
    depth = w_in.shape[0]
    m = batch * seq
    attn_w = N_HEADS * HEAD_DIM
    kv_w = N_KV_GROUPS * HEAD_DIM
    conv_ch = w_dw.shape[-1]
    c_conv = attn_w + 6 * kv_w
    c_gate = c_conv + 2 * conv_ch
    consts = _nsa_constants(rel_bias, seq)
    xf = x.reshape(m, d).astype(F32)
    row = lambda a: a.reshape(1, -1).astype(F32)
    w_in_bf = w_in[0].astype(BF16)
    for l in range(depth):
        h = _rmsnorm(xf, norm_mix[l], BF16)
        attn, w_attn_bf = _nsa(h, w_in_bf, w_in[l, :, w_in.shape[-1] - 3 * N_HEADS:], (w_attn_out, l),
                               cmp_pos[l], cmp_w1[l], cmp_w2[l], consts, batch, seq)
        glu, w_conv_bf = _mm_call(_mm_glu_kernel, [h], [_panel(w_in_bf, 0, c_conv), _panel(w_in_bf, 0, c_conv + conv_ch)],
                                  [], [row(b_glu[l, :conv_ch]), row(b_glu[l, conv_ch:])], F32, "proj_conv_glu",
                                  n=conv_ch, side_casts=[(w_conv_out, l)])
        conv = _conformer_conv(glu.reshape(batch, seq, conv_ch), w_dw[l], b_dw[l], conv_ln_g[l], conv_ln_b[l])
        gate_ab, w_out_bf = _mm_call(functools.partial(_mm_scale_kernel, scale=1.0), [h], [_panel(w_in_bf, 0, c_gate)],
                                     [], [], F32, "proj_merge_gates", n=2 * d, side_casts=[(w_out, l)])
        nxt = [(w_in, l + 1)] if l + 1 < depth else []
        merged, w_gate_bf, *w_in_next = _mm_call(_mm_merge_kernel, [attn, conv.reshape(m, conv_ch)],
                                                 [_panel(w_attn_bf), _panel(w_conv_bf)],
                                                 [(gate_ab, 0), (gate_ab, d)], [row(b_conv_out[l])], BF16, "merge",
                                                 side_casts=[(w_ffn_gate, l)] + nxt)
        xf, w_up_bf = _mm_call(_mm_residual_kernel, [merged], [_panel(w_out_bf)], [(xf, 0)], [], F32, "mix_out",
                               side_casts=[(w_ffn_up, l)])
        h = _rmsnorm(xf, norm_ffn[l], BF16)
        gu, w_down_bf = _mm_call(_mm_swiglu_kernel, [h], [_panel(w_gate_bf), _panel(w_up_bf)], [], [], BF16,
                                 "ffn_swiglu", side_casts=[(w_ffn_down, l)])
        xf = _mm_call(_mm_residual_kernel, [gu], [_panel(w_down_bf)], [(xf, 0)], [], F32, "ffn_down")
        w_in_bf = w_in_next[0] if nxt else None
    out = _rmsnorm(xf, norm_final, F32)
    return out.reshape(batch, seq, d).astype(x.dtype)
```

```python
import functools
import math

import jax
import jax.numpy as jnp
import numpy as np
from jax import lax
from jax.experimental import pallas as pl
from jax.experimental.pallas import tpu as pltpu

F32 = jnp.float32
BF16 = jnp.bfloat16

N_HEADS = 32
HEAD_DIM = 128
N_KV_GROUPS = 4
HEADS_PER_GROUP = N_HEADS // N_KV_GROUPS
CMP_BLOCK = 32
CMP_STRIDE = 16
SEL_BLOCK = 64
LOG2_SEL_BLOCK = 6
SEL_TOPK = 16
WINDOW = 512
CONV_WIDTH = 31
NUM_BUCKETS = 32
MAX_EXACT = NUM_BUCKETS // 2
MAX_DISTANCE = 128
EPS = 1e-6
NEG_INF = -1e30

LANES = 128
SUBLANES = 8
VMEM_LIMIT_BYTES = 56 * 1024 * 1024
MM_VMEM_BUDGET_BYTES = 50 * 1024 * 1024
MM_STEP_OVERHEAD_S = 0.5e-6
MM_PANEL_FETCH_BYTES_PER_S = 3.0e12
LOG2_E = math.log2(math.e)

Q_TILE = 128
FAR_CHUNK = 512
NEAR_KEYS = 2 * Q_TILE
WIN_KEYS = WINDOW + Q_TILE
CMP_HEADS = 8
FAR_HEADS = 1
NEAR_HEADS = 8
WIN_HEADS = 4
CONV_HALO = 32
CONV_LANE_CHUNK = 512
CONV_ROW_BLOCK = 32


def _params(*sem):
    return pltpu.CompilerParams(dimension_semantics=sem, vmem_limit_bytes=VMEM_LIMIT_BYTES)


def _rmsnorm_kernel(x_ref, g_ref, o_ref):
    x = x_ref[...]
    ms = jnp.mean(x * x, axis=-1, keepdims=True)
    o_ref[...] = (x * lax.rsqrt(ms + EPS) * g_ref[...]).astype(o_ref.dtype)


def _rmsnorm(x, g, out_dtype, tm=256):
    m, d = x.shape
    return pl.pallas_call(
        _rmsnorm_kernel,
        grid=(m // tm,),
        in_specs=[pl.BlockSpec((tm, d), lambda i: (i, 0)),
                  pl.BlockSpec((1, d), lambda i: (0, 0))],
        out_specs=pl.BlockSpec((tm, d), lambda i: (i, 0)),
        out_shape=jax.ShapeDtypeStruct((m, d), out_dtype),
        compiler_params=_params("parallel"),
        name="rmsnorm",
    )(x, g.reshape(1, d).astype(F32))


def _dot(a, b):
    return jnp.dot(a, b, preferred_element_type=F32)


def _mm_scale_kernel(a_ref, b_ref, o_ref, *, scale):
    acc = _dot(a_ref[...], b_ref[...])
    if scale != 1.0:
        acc = acc * scale
    o_ref[...] = acc.astype(o_ref.dtype)


def _mm_glu_kernel(a_ref, b1_ref, b2_ref, v1_ref, v2_ref, o_ref):
    a = a_ref[...]
    u = _dot(a, b1_ref[...]) + v1_ref[...]
    g = _dot(a, b2_ref[...]) + v2_ref[...]
    o_ref[...] = (u * jax.nn.sigmoid(g)).astype(o_ref.dtype)


def _mm_swiglu_kernel(a_ref, b1_ref, b2_ref, o_ref):
    a = a_ref[...]
    g = _dot(a, b1_ref[...])
    u = _dot(a, b2_ref[...])
    o_ref[...] = (jax.nn.silu(g) * u).astype(o_ref.dtype)


def _mm_merge_kernel(a1_ref, a2_ref, b1_ref, b2_ref, ga_ref, gb_ref, v_ref, o_ref):
    ya = _dot(a1_ref[...], b1_ref[...])
    yb = _dot(a2_ref[...], b2_ref[...]) + v_ref[...]
    o_ref[...] = (jax.nn.sigmoid(ga_ref[...]) * ya + jax.nn.sigmoid(gb_ref[...]) * yb).astype(o_ref.dtype)


def _mm_residual_kernel(a_ref, b_ref, r_ref, o_ref):
    o_ref[...] = r_ref[...] + _dot(a_ref[...], b_ref[...])


def _mm_tiles(m, n, a_ks, b_ks, n_mn, out_itemsize, col_starts, side_shapes):
    bf16_bytes, f32_bytes, double = 2, 4, 2
    best = None
    for tm in (2048, 1024, 512, 256, 128):
        if m % tm:
            continue
        for tn in (1024, 512, 256, 128):
            if n % tn or any(c % tn for c in col_starts):
                continue
            steps = (m // tm) * (n // tn)
            if any(r % (steps * 2 * SUBLANES) for r, _ in side_shapes):
                continue
            fixed = (sum(k * tn * bf16_bytes * double for k in b_ks)
                     + n_mn * tm * tn * f32_bytes * double
                     + tm * tn * out_itemsize * double
                     + (len(b_ks) + 1) * tm * tn * f32_bytes
                     + sum(r // steps * c * (f32_bytes + bf16_bytes) * double for r, c in side_shapes))
            a_once = sum(tm * k * bf16_bytes for k in a_ks)
            for a_bufs in (2, 1):
                if fixed + a_bufs * a_once <= MM_VMEM_BUDGET_BYTES:
                    cost = (m // tm) * (n // tn) * MM_STEP_OVERHEAD_S
                    if a_bufs == 1:
                        cost += (m // tm) * a_once / MM_PANEL_FETCH_BYTES_PER_S
                    key = (cost, -tm)
                    if best is None or key < best[0]:
                        best = (key, tm, tn, a_bufs)
                    break
    assert best is not None, (m, n, a_ks, b_ks)
    return best[1:]


def _panel(w, layer=0, col0=0):
    return (w, layer, col0)


def _with_side_casts(body, n_in, n_side):
    def wrapped(*refs):
        body(*refs[:n_in], refs[n_in + n_side])
        for src, dst in zip(refs[n_in:n_in + n_side], refs[n_in + n_side + 1:]):
            dst[...] = src[...].astype(dst.dtype)
    return wrapped


def _mm_call(body, a_ops, b_ops, mn_ops, vec_ops, out_dtype, name, n=None, side_casts=()):
    m = a_ops[0].shape[0]
    n = b_ops[0][0].shape[-1] if n is None else n
    a_ks = [a.shape[1] for a in a_ops]
    b_ks = [a_ks[min(idx, len(a_ks) - 1)] for idx in range(len(b_ops))]
    col_starts = [c for _, _, c in b_ops] + [c for _, c in mn_ops]
    tm, tn, a_bufs = _mm_tiles(m, n, a_ks, b_ks, len(mn_ops), jnp.dtype(out_dtype).itemsize, col_starts,
                               [s.shape[1:] for s, _ in side_casts])
    steps_n = n // tn
    a_mode = {} if a_bufs == 2 else dict(pipeline_mode=pl.Buffered(1))
    in_specs = []
    for a in a_ops:
        in_specs.append(pl.BlockSpec((tm, a.shape[1]), lambda i, j: (i, 0), **a_mode))
    for (w, layer, col0), k in zip(b_ops, b_ks):
        assert col0 % tn == 0 and w.shape[-2] == k and col0 + n <= w.shape[-1]
        if w.ndim == 3:
            in_specs.append(pl.BlockSpec((None, k, tn), lambda i, j, l=layer, off=col0 // tn: (l, 0, j + off)))
        else:
            in_specs.append(pl.BlockSpec((k, tn), lambda i, j, off=col0 // tn: (0, j + off)))
    b_ops = [w for w, _, _ in b_ops]
    for _, col0 in mn_ops:
        assert col0 % tn == 0
        in_specs.append(pl.BlockSpec((tm, tn), lambda i, j, off=col0 // tn: (i, j + off)))
    mn_ops = [a for a, _ in mn_ops]
    for _ in vec_ops:
        in_specs.append(pl.BlockSpec((1, tn), lambda i, j: (0, j)))
    out_specs = [pl.BlockSpec((tm, tn), lambda i, j: (i, j))]
    out_shape = [jax.ShapeDtypeStruct((m, n), out_dtype)]
    steps = (m // tm) * steps_n
    for s, layer in side_casts:
        _, rows, cols = s.shape
        slab = rows // steps
        assert slab * steps == rows and slab % (2 * SUBLANES) == 0, (name, s.shape, steps)
        in_specs.append(pl.BlockSpec((None, slab, cols), lambda i, j, l=layer: (l, i * steps_n + j, 0)))
        out_specs.append(pl.BlockSpec((slab, cols), lambda i, j: (i * steps_n + j, 0)))
        out_shape.append(jax.ShapeDtypeStruct((rows, cols), BF16))
    n_in = len(in_specs) - len(side_casts)
    if side_casts:
        body = _with_side_casts(body, n_in, len(side_casts))
    res = pl.pallas_call(
        body,
        grid=(m // tm, steps_n),
        in_specs=in_specs,
        out_specs=out_specs,
        out_shape=out_shape,
        compiler_params=_params("arbitrary" if side_casts else "parallel", "arbitrary"),
        name=name,
    )(*a_ops, *b_ops, *mn_ops, *vec_ops, *[s for s, _ in side_casts])
    return tuple(res) if side_casts else res[0]


def _gelu_tanh(x):
    c = math.sqrt(2.0 / math.pi)
    return x * (0.5 * (1.0 + jnp.tanh(c * (x + 0.044715 * (x * x * x)))))


def _compress_kernel(x_ref, pos_ref, w1_ref, w2_ref, o_ref):
    t, dh = x_ref.shape[1], x_ref.shape[2]
    ns = t // CMP_STRIDE
    p1 = jnp.zeros((ns, dh), F32)
    p2 = jnp.zeros((ns, dh), F32)
    for i in range(CMP_STRIDE):
        xi = x_ref[0, pl.ds(i, ns, stride=CMP_STRIDE), :]
        p1 = p1 + _dot((xi + pos_ref[0, i:i + 1]).astype(BF16), w1_ref[0, i])
        p2 = p2 + _dot((xi + pos_ref[0, CMP_STRIDE + i:CMP_STRIDE + i + 1]).astype(BF16), w1_ref[0, CMP_STRIDE + i])
    pre = p1 + pltpu.roll(p2, ns - 1, 0)
    h = _gelu_tanh(pre).astype(BF16)
    o_ref[0, 0] = _dot(h, w2_ref[0]).astype(o_ref.dtype)


def _compress(kvc, pos, w1, w2):
    b, t, _ = kvc.shape
    dh = w2.shape[-1]
    g = N_KV_GROUPS
    ns = t // CMP_STRIDE
    return pl.pallas_call(
        _compress_kernel,
        grid=(2, b, g),
        in_specs=[pl.BlockSpec((1, t, dh), lambda k, bi, gi: (bi, 0, k * N_KV_GROUPS + gi)),
                  pl.BlockSpec((1, CMP_BLOCK, dh), lambda k, bi, gi: (k, 0, 0)),
                  pl.BlockSpec((1, CMP_BLOCK, dh, dh), lambda k, bi, gi: (k, 0, 0, 0)),
                  pl.BlockSpec((1, dh, dh), lambda k, bi, gi: (k, 0, 0))],
        out_specs=pl.BlockSpec((1, 1, ns, dh), lambda k, bi, gi: (k, bi * N_KV_GROUPS + gi, 0, 0)),
        out_shape=jax.ShapeDtypeStruct((2, b * g, ns, dh), BF16),
        compiler_params=_params("parallel", "parallel", "parallel"),
        name="nsa_compress",
    )(kvc, pos, w1, w2)


def _nt_dot(a, b):
    return lax.dot_general(a, b, (((1,), (1,)), ((), ())), preferred_element_type=F32)


def _nsa_kernel(q_ref, kslc_ref, vslc_ref, kwin_ref, vwin_ref, kcmp_ref, vcmp_ref, gates_ref,
                bnear_ref, bcmp_ref, emat_ref, selmap_t_ref, o_ref,
                m_ref, acc_ref):
    i = pl.program_id(2)
    tq = Q_TILE
    nh = HEADS_PER_GROUP
    dh = HEAD_DIM
    rows = nh * tq
    n_cmp = kcmp_ref.shape[2]

    def q_block(r0, hb):
        return jnp.concatenate([q_ref[0, :, r * dh:(r + 1) * dh] for r in range(r0, r0 + hb)], axis=0)

    def stack_heads(ref, r0, hb):
        return ref[r0:r0 + hb].reshape(hb * tq, ref.shape[-1])

    def block_rows(r0, hb):
        return slice(r0 * tq, (r0 + hb) * tq)

    def add_shared(s, shared):
        hb, n = s.shape[0] // tq, s.shape[1]
        return (s.reshape(hb, tq, n) + shared[None]).reshape(hb * tq, n)

    lane = lax.broadcasted_iota(jnp.int32, (tq, LANES), 1)

    nrow = lax.broadcasted_iota(jnp.int32, (n_cmp, LANES), 0)
    ncol = lax.broadcasted_iota(jnp.int32, (n_cmp, LANES), 1)
    place = jnp.where((nrow == (ncol & 15) + (8 * i - 8)) & (ncol < 48), 1.0, 0.0).astype(BF16)
    k_aug = jnp.concatenate([kcmp_ref[0, 0], place], axis=1)
    tc = lax.broadcasted_iota(jnp.int32, (tq, n_cmp), 0) + i * tq
    nc = lax.broadcasted_iota(jnp.int32, (tq, n_cmp), 1)
    ok_c = (tc - (nc * CMP_STRIDE + CMP_BLOCK - 1)) >= 0
    mask_c = jnp.where(ok_c, 0.0, NEG_INF)
    v_cmp = vcmp_ref[0, 0]
    o_cmp = []
    psum = jnp.zeros((tq, n_cmp), F32)
    for r0 in range(0, nh, CMP_HEADS):
        q_aug = jnp.concatenate([q_block(r0, CMP_HEADS), stack_heads(bcmp_ref, r0, CMP_HEADS)], axis=1)
        s = add_shared(_nt_dot(q_aug, k_aug), mask_c)
        e = jnp.exp2(s - jnp.max(s, axis=-1, keepdims=True))
        p = e / jnp.sum(e, axis=-1, keepdims=True)
        p3 = jnp.where(ok_c[None], p.reshape(CMP_HEADS, tq, n_cmp), 0.0)
        o_cmp.append(_dot(p3.reshape(CMP_HEADS * tq, n_cmp).astype(BF16), v_cmp))
        psum = psum + jnp.sum(p3, axis=0)
    o_cmp = jnp.concatenate(o_cmp, axis=0)

    p_hi = psum.astype(BF16)
    p_lo = (psum - p_hi.astype(F32)).astype(BF16)
    imp_t = _nt_dot(selmap_t_ref[...], p_hi) + _nt_dot(selmap_t_ref[...], p_lo)
    n_blocks = emat_ref.shape[1] // SEL_BLOCK
    assert n_blocks <= LANES and n_blocks % SUBLANES == 0
    blk = lax.broadcasted_iota(jnp.int32, (n_blocks, tq), 0)
    cur_t = (lax.broadcasted_iota(jnp.int32, (n_blocks, tq), 1) + i * tq) >> LOG2_SEL_BLOCK
    valid_t = blk <= cur_t
    forced_t = (blk == 0) | (blk == cur_t) | (blk == cur_t - 1)
    big = 1e30
    x = jnp.where(valid_t, jnp.where(forced_t, big, imp_t[:n_blocks]), -big)
    bcast = [jnp.broadcast_to(x[j:j + 1], (SUBLANES, tq)) for j in range(n_blocks)]
    cnt_groups = []
    for g0 in range(0, n_blocks, SUBLANES):
        xg = x[g0:g0 + SUBLANES]
        blk_g = blk[g0:g0 + SUBLANES]
        cnt = jnp.zeros((SUBLANES, tq), F32)
        for j in range(n_blocks):
            if j < g0:
                beats = bcast[j] >= xg
            elif j >= g0 + SUBLANES:
                beats = bcast[j] > xg
            else:
                beats = (bcast[j] > xg) | ((bcast[j] == xg) & (blk_g > j))
            cnt = cnt + jnp.where(beats, 1.0, 0.0)
        cnt_groups.append(cnt)
    cnt = jnp.concatenate(cnt_groups, axis=0)
    sel_t = jnp.where(valid_t & (cnt < SEL_TOPK), 1.0, 0.0)
    sel_t = jnp.concatenate([sel_t, jnp.zeros((LANES - n_blocks, tq), F32)], axis=0)
    sel = sel_t.T

    def lane_tile(x, n):
        return x if n == 1 else jnp.concatenate([x] * n, axis=1)

    def with_ones(v):
        return jnp.concatenate([v, jnp.ones_like(v)], axis=1)

    m_ref[...] = jnp.full(m_ref.shape, NEG_INF, F32)
    acc_ref[...] = jnp.zeros(acc_ref.shape, F32)

    def online_update(sl, s, v_ones):
        m_prev = m_ref[sl]
        m_next = jnp.maximum(m_prev, jnp.max(s, axis=-1, keepdims=True))
        alpha = jnp.exp2(m_prev - m_next)
        p = jnp.exp2(s - lane_tile(m_next, s.shape[-1] // LANES))
        acc_ref[sl] = lane_tile(alpha, 2) * acc_ref[sl] + _dot(p.astype(BF16), v_ones)
        m_ref[sl] = m_next

    sel_far = jnp.where(lane <= 2 * i - 3, sel, 0.0).astype(BF16)

    def far_body(c, carry):
        c0 = pl.multiple_of(c * FAR_CHUNK, FAR_CHUNK)
        k = kslc_ref[0, pl.ds(c0 + Q_TILE, FAR_CHUNK), :]
        v = with_ones(vslc_ref[0, pl.ds(c0 + Q_TILE, FAR_CHUNK), :])
        hit = _dot(sel_far, emat_ref[:, pl.ds(c0, FAR_CHUNK)])
        mask = (hit - 1.0) * (-NEG_INF)
        for h0 in range(0, nh, FAR_HEADS):
            online_update(block_rows(h0, FAR_HEADS), add_shared(_nt_dot(q_block(h0, FAR_HEADS), k), mask), v)
        return carry

    tiles_per_chunk = FAR_CHUNK // Q_TILE
    lax.fori_loop(0, (i + tiles_per_chunk - 2) // tiles_per_chunk, far_body, 0)

    r0 = pl.multiple_of(i * tq, tq)
    k = kslc_ref[0, pl.ds(r0, NEAR_KEYS), :]
    v = with_ones(vslc_ref[0, pl.ds(r0, NEAR_KEYS), :])
    erow = lax.broadcasted_iota(jnp.int32, (LANES, NEAR_KEYS), 0)
    ecol = lax.broadcasted_iota(jnp.int32, (LANES, NEAR_KEYS), 1)
    e_near = jnp.where(erow == 2 * i - 2 + (ecol >> LOG2_SEL_BLOCK), 1.0, 0.0).astype(BF16)
    hit = _dot(sel.astype(BF16), e_near)
    mask = (hit - 1.0) * (-NEG_INF)
    for h0 in range(0, nh, NEAR_HEADS):
        s = add_shared(_nt_dot(q_block(h0, NEAR_HEADS), k), mask) + stack_heads(bnear_ref, h0, NEAR_HEADS)
        online_update(block_rows(h0, NEAR_HEADS), s, v)

    k_w = kwin_ref[0, pl.ds(r0, WIN_KEYS), :]
    v_w = with_ones(vwin_ref[0, pl.ds(r0, WIN_KEYS), :])
    wrow = lax.broadcasted_iota(jnp.int32, (tq, WINDOW), 0)
    wcol = lax.broadcasted_iota(jnp.int32, (tq, WINDOW), 1)
    ok_w = (wcol >= WINDOW - i * tq) & ((wcol >= tq) | (wcol > wrow))
    mask_w = jnp.where(ok_w, 0.0, NEG_INF)
    old = WIN_KEYS - NEAR_KEYS
    gates = jax.nn.sigmoid(gates_ref[0, 0])
    for h0 in range(0, nh, WIN_HEADS):
        s = _nt_dot(q_block(h0, WIN_HEADS), k_w)
        bias = stack_heads(bnear_ref, h0, WIN_HEADS)
        s = jnp.concatenate([add_shared(s[:, :old], mask_w[:, :old]),
                             add_shared(s[:, old:WINDOW], mask_w[:, old:]) + bias[:, :WINDOW - old],
                             s[:, WINDOW:] + bias[:, WINDOW - old:]], axis=1)
        pv = _dot(jnp.exp2(s - jnp.max(s, axis=-1, keepdims=True)).astype(BF16), v_w)
        o_win = pv[:, :dh] / pv[:, dh:]
        acc = acc_ref[block_rows(h0, WIN_HEADS)]
        o_slc = acc[:, :dh] / acc[:, dh:]
        for r in range(h0, h0 + WIN_HEADS):
            sl = block_rows(r - h0, 1)
            o = (gates[:, 3 * r:3 * r + 1] * o_cmp[block_rows(r, 1)]
                 + gates[:, 3 * r + 1:3 * r + 2] * o_slc[sl]
                 + gates[:, 3 * r + 2:3 * r + 3] * o_win[sl])
            o_ref[0, :, r * dh:(r + 1) * dh] = o.astype(o_ref.dtype)


def _nsa_attention(q, kslc, vslc, kwin, vwin, kvcmp, gates, bnear, bcmp, emat, selmap_t):
    b, t, _ = q.shape
    g = N_KV_GROUPS
    nh = HEADS_PER_GROUP
    dh = HEAD_DIM
    tq = Q_TILE
    n_cmp = kvcmp.shape[2]
    tp_slc = kslc.shape[1]
    tp_win = kwin.shape[1]
    rows = nh * tq
    kv_spec = lambda tp, w: pl.BlockSpec((1, tp, w), lambda bi, gi, ti: (bi, 0, gi))
    return pl.pallas_call(
        _nsa_kernel,
        grid=(b, g, t // tq),
        in_specs=[
            pl.BlockSpec((1, tq, nh * dh), lambda bi, gi, ti: (bi, ti, gi)),
            kv_spec(tp_slc, dh), kv_spec(tp_slc, dh), kv_spec(tp_win, dh), kv_spec(tp_win, dh),
            pl.BlockSpec((1, 1, n_cmp, dh), lambda bi, gi, ti: (0, bi * N_KV_GROUPS + gi, 0, 0)),
            pl.BlockSpec((1, 1, n_cmp, dh), lambda bi, gi, ti: (1, bi * N_KV_GROUPS + gi, 0, 0)),
            pl.BlockSpec((1, 1, tq, LANES), lambda bi, gi, ti: (bi, gi, ti, 0)),
            pl.BlockSpec((nh, tq, NEAR_KEYS), lambda bi, gi, ti: (gi, 0, 0)),
            pl.BlockSpec((nh, tq, LANES), lambda bi, gi, ti: (gi, 0, 0)),
            pl.BlockSpec(emat.shape, lambda bi, gi, ti: (0, 0)),
            pl.BlockSpec(selmap_t.shape, lambda bi, gi, ti: (0, 0)),
        ],
        out_specs=pl.BlockSpec((1, tq, nh * dh), lambda bi, gi, ti: (bi, ti, gi)),
        out_shape=jax.ShapeDtypeStruct((b, t, g * nh * dh), BF16),
        scratch_shapes=[pltpu.VMEM((rows, LANES), F32),
                        pltpu.VMEM((rows, 2 * dh), F32)],
        compiler_params=_params("parallel", "parallel", "arbitrary"),
        name="nsa_attention",
    )(q, kslc, vslc, kwin, vwin, kvcmp, kvcmp, gates, bnear, bcmp, emat, selmap_t)


def _t5_bucket_table(max_dist):
    d = np.arange(max_dist)
    nf = np.maximum(d, MAX_EXACT).astype(np.float32)
    large = MAX_EXACT + (np.log(nf / np.float32(MAX_EXACT)) / np.float32(math.log(MAX_DISTANCE / MAX_EXACT))
                         * np.float32(NUM_BUCKETS - MAX_EXACT)).astype(np.int32)
    large = np.minimum(large, NUM_BUCKETS - 1)
    return np.where(d < MAX_EXACT, d, large).astype(np.int32)


def _nsa_constants(rel_bias, seq):
    tq = Q_TILE
    max_d = NEAR_KEYS
    bucket = _t5_bucket_table(max_d)
    far_cmp = tq - (CMP_BLOCK - 1) + CMP_STRIDE
    assert np.all(bucket[min(far_cmp, tq + 1):] == NUM_BUCKETS - 1)
    table = rel_bias.astype(F32).T * LOG2_E
    delta = table[:, bucket] - table[:, NUM_BUCKETS - 1:]

    strip = jnp.concatenate([delta[:, ::-1], jnp.full((N_HEADS, tq - 1), NEG_INF, F32)], axis=1)
    bnear = jnp.stack([strip[:, tq - 1 - t:tq - 1 - t + NEAR_KEYS] for t in range(tq)], axis=1)
    t = np.arange(tq)[:, None]
    npr = np.arange(16)[None, :]
    d_cmp = t - CMP_STRIDE * npr + (8 * CMP_STRIDE - (CMP_BLOCK - 1))
    bc = jnp.where(jnp.asarray(d_cmp >= 0)[None], delta[:, np.clip(d_cmp, 0, max_d - 1)], 0.0)
    hi = bc.astype(BF16)
    mid = (bc - hi.astype(F32)).astype(BF16)
    lo = (bc - hi.astype(F32) - mid.astype(F32)).astype(BF16)
    bcmp = jnp.concatenate([hi, mid, lo, jnp.zeros((N_HEADS, tq, LANES - 48), BF16)], axis=-1)

    nsb = seq // SEL_BLOCK
    emat = np.zeros((LANES, seq), np.float32)
    emat[np.arange(seq) // SEL_BLOCK, np.arange(seq)] = 1.0
    ns = seq // CMP_STRIDE
    c_start = np.arange(ns) * CMP_STRIDE
    s_start = np.arange(nsb) * SEL_BLOCK
    ov = np.minimum(c_start[:, None] + CMP_BLOCK, s_start[None, :] + SEL_BLOCK) - np.maximum(c_start[:, None], s_start[None, :])
    ov = np.clip(ov, 0, None) // CMP_STRIDE
    selmap_t = np.zeros((LANES, ns), np.float32)
    selmap_t[:nsb, :ns - 1] = ov[:ns - 1].T
    return bnear, bcmp, jnp.asarray(emat, BF16), jnp.asarray(selmap_t, BF16)


def _conv_kernel(cur_ref, prev_ref, w_ref, b_ref, g_ref, beta_ref, o_ref, shift_ref, conv_ref):
    tt, c = cur_ref.shape[1], cur_ref.shape[2]
    first = pl.program_id(1) == 0
    base = CONV_HALO - (CONV_WIDTH - 1)
    groups = CONV_ROW_BLOCK // SUBLANES

    def chunk_body(ci, carry):
        lanes = pl.ds(pl.multiple_of(ci * CONV_LANE_CHUNK, CONV_LANE_CHUNK), CONV_LANE_CHUNK)
        halo = jnp.where(first, 0.0, prev_ref[0, :, lanes])
        xin = jnp.concatenate([halo, cur_ref[0, :, lanes]], axis=0)
        shift_ref[0] = xin
        keep = tt + CONV_HALO - SUBLANES
        for s in range(1, SUBLANES):
            shift_ref[s, 0:keep, :] = xin[s:s + keep]
        bias = b_ref[:, lanes]
        for r0 in range(0, tt, CONV_ROW_BLOCK):
            acc = jnp.zeros((groups, SUBLANES, CONV_LANE_CHUNK), F32) + bias
            for k in range(CONV_WIDTH):
                s = (base + k) % SUBLANES
                a = r0 + base + k - s
                win = shift_ref[s, a:a + CONV_ROW_BLOCK, :].reshape(groups, SUBLANES, CONV_LANE_CHUNK)
                acc = acc + w_ref[k, :, lanes][None] * win
            conv_ref[r0:r0 + CONV_ROW_BLOCK, lanes] = acc.reshape(CONV_ROW_BLOCK, CONV_LANE_CHUNK)
        return carry

    lax.fori_loop(0, c // CONV_LANE_CHUNK, chunk_body, 0)

    acc = conv_ref[...]
    mu = jnp.mean(acc, axis=-1, keepdims=True)
    xc = acc - mu
    var = jnp.mean(xc * xc, axis=-1, keepdims=True)
    y = xc * lax.rsqrt(var + EPS) * g_ref[...] + beta_ref[...]
    o_ref[0] = jax.nn.silu(y).astype(o_ref.dtype)


def _conformer_conv(h, w_dw, b_dw, ln_g, ln_b, tt=256):
    b, t, c = h.shape
    tt = min(tt, t)
    hb = tt // CONV_HALO
    assert c % CONV_LANE_CHUNK == 0 and tt % CONV_ROW_BLOCK == 0 and t % tt == 0
    wrep = jnp.broadcast_to(w_dw.astype(F32)[:, None, :], (CONV_WIDTH, SUBLANES, c))
    vec = lambda a: a.reshape(1, c).astype(F32)
    return pl.pallas_call(
        _conv_kernel,
        grid=(b, t // tt),
        in_specs=[pl.BlockSpec((1, tt, c), lambda bi, ti: (bi, ti, 0)),
                  pl.BlockSpec((1, CONV_HALO, c), lambda bi, ti: (bi, jnp.maximum(ti * hb - 1, 0), 0)),
                  pl.BlockSpec((CONV_WIDTH, SUBLANES, c), lambda bi, ti: (0, 0, 0)),
                  pl.BlockSpec((1, c), lambda bi, ti: (0, 0)),
                  pl.BlockSpec((1, c), lambda bi, ti: (0, 0)),
                  pl.BlockSpec((1, c), lambda bi, ti: (0, 0))],
        out_specs=pl.BlockSpec((1, tt, c), lambda bi, ti: (bi, ti, 0)),
        out_shape=jax.ShapeDtypeStruct((b, t, c), BF16),
        scratch_shapes=[pltpu.VMEM((SUBLANES, CONV_HALO + tt, CONV_LANE_CHUNK), F32),
                        pltpu.VMEM((tt, c), F32)],
        compiler_params=_params("parallel", "arbitrary"),
        name="conformer_conv",
    )(h, h, wrep, vec(b_dw), vec(ln_g), vec(ln_b))


def _nsa(h, w_in_bf, layer, w_gates_l, w_attn_out_side, cmp_pos, cmp_w1, cmp_w2, consts, batch, seq):
    dh = HEAD_DIM
    g = N_KV_GROUPS
    attn_w = N_HEADS * dh
    kv_w = g * dh
    m = h.shape[0]
    c0 = attn_w
    plain = functools.partial(_mm_scale_kernel, scale=1.0)
    q, w_attn_bf = _mm_call(functools.partial(_mm_scale_kernel, scale=dh ** -0.5 * LOG2_E), [h],
                            [_panel(w_in_bf, layer)], [], [], BF16, "proj_q", n=attn_w, side_casts=[w_attn_out_side])
    kvc = _mm_call(plain, [h], [_panel(w_in_bf, layer, c0)], [], [], F32, "proj_kv_cmp", n=2 * kv_w)
    kvr = _mm_call(plain, [h], [_panel(w_in_bf, layer, c0 + 2 * kv_w)], [], [], BF16, "proj_kv", n=4 * kv_w)
    n_gates = 3 * N_HEADS
    w_br = jnp.pad(w_gates_l, ((0, 0), (0, LANES - n_gates))).astype(BF16)
    br = _mm_call(plain, [h], [_panel(w_br)], [], [], F32, "proj_branch_gates")

    w1 = cmp_w1.astype(BF16).reshape(2, CMP_BLOCK, dh, dh)
    kvcmp = _compress(kvc.reshape(batch, seq, 2 * kv_w), cmp_pos.astype(F32), w1, cmp_w2.astype(BF16))

    kvr = kvr.reshape(batch, seq, 4 * kv_w)
    padt = lambda a, n: jnp.pad(a, ((0, 0), (n, 0), (0, 0)))
    kslc = padt(kvr[:, :, 0:kv_w], Q_TILE)
    vslc = padt(kvr[:, :, kv_w:2 * kv_w], Q_TILE)
    kwin = padt(kvr[:, :, 2 * kv_w:3 * kv_w], WINDOW)
    vwin = padt(kvr[:, :, 3 * kv_w:4 * kv_w], WINDOW)
    per_g = 3 * HEADS_PER_GROUP
    gates = br[:, :n_gates].reshape(batch, seq, g, per_g).transpose(0, 2, 1, 3)
    gates = jnp.pad(gates, ((0, 0), (0, 0), (0, 0), (0, LANES - per_g)))
    o = _nsa_attention(q.reshape(batch, seq, attn_w), kslc, vslc, kwin, vwin, kvcmp, gates, *consts)
    return o.reshape(m, attn_w), w_attn_bf


def kernel(x, w_in, cmp_pos, cmp_w1, cmp_w2, rel_bias, w_attn_out, b_glu, w_dw, b_dw, conv_ln_g, conv_ln_b, w_conv_out, b_conv_out, w_out, norm_mix, norm_ffn, w_ffn_gate, w_ffn_up, w_ffn_down, norm_final):
    batch, seq, d = x.shape
    depth = w_in.shape[0]
    m = batch * seq
    attn_w = N_HEADS * HEAD_DIM
    kv_w = N_KV_GROUPS * HEAD_DIM
    conv_ch = w_dw.shape[-1]
    c_conv = attn_w + 6 * kv_w
    c_gate = c_conv + 2 * conv_ch
    consts = _nsa_constants(rel_bias, seq)
    xf = x.reshape(m, d).astype(F32)
    row = lambda a: a.reshape(1, -1).astype(F32)
    w_in_bf = w_in.astype(BF16)
    for l in range(depth):
        h = _rmsnorm(xf, norm_mix[l], BF16)
        attn, w_attn_bf = _nsa(h, w_in_bf, l, w_in[l, :, w_in.shape[-1] - 3 * N_HEADS:], (w_attn_out, l),
                               cmp_pos[l], cmp_w1[l], cmp_w2[l], consts, batch, seq)
        glu, w_conv_bf = _mm_call(_mm_glu_kernel, [h], [_panel(w_in_bf, l, c_conv), _panel(w_in_bf, l, c_conv + conv_ch)],
                                  [], [row(b_glu[l, :conv_ch]), row(b_glu[l, conv_ch:])], F32, "proj_conv_glu",
                                  n=conv_ch, side_casts=[(w_conv_out, l)])
        conv = _conformer_conv(glu.reshape(batch, seq, conv_ch), w_dw[l], b_dw[l], conv_ln_g[l], conv_ln_b[l])
        gate_ab, w_out_bf = _mm_call(functools.partial(_mm_scale_kernel, scale=1.0), [h], [_panel(w_in_bf, l, c_gate)],
                                     [], [], F32, "proj_merge_gates", n=2 * d, side_casts=[(w_out, l)])
        merged, w_gate_bf = _mm_call(_mm_merge_kernel, [attn, conv.reshape(m, conv_ch)],
                                     [_panel(w_attn_bf), _panel(w_conv_bf)],
                                     [(gate_ab, 0), (gate_ab, d)], [row(b_conv_out[l])], BF16, "merge",
                                     side_casts=[(w_ffn_gate, l)])
        xf, w_up_bf = _mm_call(_mm_residual_kernel, [merged], [_panel(w_out_bf)], [(xf, 0)], [], F32, "mix_out",
                               side_casts=[(w_ffn_up, l)])
        h = _rmsnorm(xf, norm_ffn[l], BF16)
        gu, w_down_bf = _mm_call(_mm_swiglu_kernel, [h], [_panel(w_gate_bf), _panel(w_up_bf)], [], [], BF16,
                                 "ffn_swiglu", side_casts=[(w_ffn_down, l)])
        xf = _mm_call(_mm_residual_kernel, [gu], [_panel(w_down_bf)], [(xf, 0)], [], F32, "ffn_down")
    out = _rmsnorm(xf, norm_final, F32)
    return out.reshape(batch, seq, d).astype(x.dtype)
```

```python
import functools
import math

import jax
import jax.numpy as jnp
import numpy as np
from jax import lax
from jax.experimental import pallas as pl
from jax.experimental.pallas import tpu as pltpu

F32 = jnp.float32
BF16 = jnp.bfloat16

N_HEADS = 32
HEAD_DIM = 128
N_KV_GROUPS = 4
HEADS_PER_GROUP = N_HEADS // N_KV_GROUPS
CMP_BLOCK = 32
CMP_STRIDE = 16
SEL_BLOCK = 64
LOG2_SEL_BLOCK = 6
SEL_TOPK = 16
WINDOW = 512
CONV_WIDTH = 31
NUM_BUCKETS = 32
MAX_EXACT = NUM_BUCKETS // 2
MAX_DISTANCE = 128
EPS = 1e-6
NEG_INF = -1e30

LANES = 128
SUBLANES = 8
VMEM_LIMIT_BYTES = 56 * 1024 * 1024
MM_VMEM_BUDGET_BYTES = 51 * 1024 * 1024
MM_STEP_OVERHEAD_S = 0.5e-6
MM_PANEL_FETCH_BYTES_PER_S = 3.0e12
LOG2_E = math.log2(math.e)

Q_TILE = 128
FAR_CHUNK = 512
NEAR_KEYS = 2 * Q_TILE
WIN_KEYS = WINDOW + Q_TILE
CMP_HEADS = 8
FAR_HEADS = 1
NEAR_HEADS = 8
WIN_HEADS = 4
CONV_HALO = 32
CONV_LANE_CHUNK = 512
CONV_ROW_BLOCK = 32


def _params(*sem):
    return pltpu.CompilerParams(dimension_semantics=sem, vmem_limit_bytes=VMEM_LIMIT_BYTES)


def _rmsnorm_kernel(x_ref, g_ref, o_ref):
    x = x_ref[...]
    ms = jnp.mean(x * x, axis=-1, keepdims=True)
    o_ref[...] = (x * lax.rsqrt(ms + EPS) * g_ref[...]).astype(o_ref.dtype)


def _rmsnorm(x, g, out_dtype, tm=256):
    m, d = x.shape
    return pl.pallas_call(
        _rmsnorm_kernel,
        grid=(m // tm,),
        in_specs=[pl.BlockSpec((tm, d), lambda i: (i, 0)),
                  pl.BlockSpec((1, d), lambda i: (0, 0))],
        out_specs=pl.BlockSpec((tm, d), lambda i: (i, 0)),
        out_shape=jax.ShapeDtypeStruct((m, d), out_dtype),
        compiler_params=_params("parallel"),
        name="rmsnorm",
    )(x, g.reshape(1, d).astype(F32))


def _dot(a, b):
    return jnp.dot(a, b, preferred_element_type=F32)


def _mm_scale_kernel(a_ref, b_ref, o_ref, *, scale):
    acc = _dot(a_ref[...], b_ref[...])
    if scale != 1.0:
        acc = acc * scale
    o_ref[...] = acc.astype(o_ref.dtype)


def _mm_glu_kernel(a_ref, b1_ref, b2_ref, v1_ref, v2_ref, o_ref):
    a = a_ref[...]
    u = _dot(a, b1_ref[...]) + v1_ref[...]
    g = _dot(a, b2_ref[...]) + v2_ref[...]
    o_ref[...] = (u * jax.nn.sigmoid(g)).astype(o_ref.dtype)


def _mm_swiglu_kernel(a_ref, b1_ref, b2_ref, o_ref):
    a = a_ref[...]
    g = _dot(a, b1_ref[...])
    u = _dot(a, b2_ref[...])
    o_ref[...] = (jax.nn.silu(g) * u).astype(o_ref.dtype)


def _mm_merge_kernel(a1_ref, a2_ref, b1_ref, b2_ref, ga_ref, gb_ref, v_ref, o_ref):
    ya = _dot(a1_ref[...], b1_ref[...])
    yb = _dot(a2_ref[...], b2_ref[...]) + v_ref[...]
    o_ref[...] = (jax.nn.sigmoid(ga_ref[...]) * ya + jax.nn.sigmoid(gb_ref[...]) * yb).astype(o_ref.dtype)


def _mm_residual_kernel(a_ref, b_ref, r_ref, o_ref):
    o_ref[...] = r_ref[...] + _dot(a_ref[...], b_ref[...])


def _mm_tiles(m, n, a_ks, b_ks, n_mn, out_itemsize, col_starts, side_shapes):
    bf16_bytes, f32_bytes, double = 2, 4, 2
    best = None
    for tm in (2048, 1024, 512, 256, 128):
        if m % tm:
            continue
        for tn in (1024, 512, 256, 128):
            if n % tn or any(c % tn for c in col_starts):
                continue
            steps = (m // tm) * (n // tn)
            if any(r % (steps * 2 * SUBLANES) for r, _ in side_shapes):
                continue
            fixed = (sum(k * tn * bf16_bytes * double for k in b_ks)
                     + n_mn * tm * tn * f32_bytes * double
                     + tm * tn * out_itemsize * double
                     + (len(b_ks) + 1) * tm * tn * f32_bytes
                     + sum(r // steps * c * (f32_bytes + bf16_bytes) * double for r, c in side_shapes))
            a_once = sum(tm * k * bf16_bytes for k in a_ks)
            for a_bufs in (2, 1):
                if fixed + a_bufs * a_once <= MM_VMEM_BUDGET_BYTES:
                    cost = (m // tm) * (n // tn) * MM_STEP_OVERHEAD_S
                    if a_bufs == 1:
                        cost += (m // tm) * a_once / MM_PANEL_FETCH_BYTES_PER_S
                    key = (cost, -tm)
                    if best is None or key < best[0]:
                        best = (key, tm, tn, a_bufs)
                    break
    assert best is not None, (m, n, a_ks, b_ks)
    return best[1:]


def _panel(w, layer=0, col0=0):
    return (w, layer, col0)


def _with_side_casts(body, n_in, n_side):
    def wrapped(*refs):
        body(*refs[:n_in], refs[n_in + n_side])
        for src, dst in zip(refs[n_in:n_in + n_side], refs[n_in + n_side + 1:]):
            dst[...] = src[...].astype(dst.dtype)
    return wrapped


def _mm_call(body, a_ops, b_ops, mn_ops, vec_ops, out_dtype, name, n=None, side_casts=()):
    m = a_ops[0].shape[0]
    n = b_ops[0][0].shape[-1] if n is None else n
    a_ks = [a.shape[1] for a in a_ops]
    b_ks = [a_ks[min(idx, len(a_ks) - 1)] for idx in range(len(b_ops))]
    col_starts = [c for _, _, c in b_ops] + [c for _, c in mn_ops]
    tm, tn, a_bufs = _mm_tiles(m, n, a_ks, b_ks, len(mn_ops), jnp.dtype(out_dtype).itemsize, col_starts,
                               [s.shape[1:] for s, _ in side_casts])
    steps_n = n // tn
    a_mode = {} if a_bufs == 2 else dict(pipeline_mode=pl.Buffered(1))
    in_specs = []
    for a in a_ops:
        in_specs.append(pl.BlockSpec((tm, a.shape[1]), lambda i, j: (i, 0), **a_mode))
    for (w, layer, col0), k in zip(b_ops, b_ks):
        assert col0 % tn == 0 and w.shape[-2] == k and col0 + n <= w.shape[-1]
        if w.ndim == 3:
            in_specs.append(pl.BlockSpec((None, k, tn), lambda i, j, l=layer, off=col0 // tn: (l, 0, j + off)))
        else:
            in_specs.append(pl.BlockSpec((k, tn), lambda i, j, off=col0 // tn: (0, j + off)))
    b_ops = [w for w, _, _ in b_ops]
    for _, col0 in mn_ops:
        assert col0 % tn == 0
        in_specs.append(pl.BlockSpec((tm, tn), lambda i, j, off=col0 // tn: (i, j + off)))
    mn_ops = [a for a, _ in mn_ops]
    for _ in vec_ops:
        in_specs.append(pl.BlockSpec((1, tn), lambda i, j: (0, j)))
    out_specs = [pl.BlockSpec((tm, tn), lambda i, j: (i, j))]
    out_shape = [jax.ShapeDtypeStruct((m, n), out_dtype)]
    steps = (m // tm) * steps_n
    for s, layer in side_casts:
        _, rows, cols = s.shape
        slab = rows // steps
        assert slab * steps == rows and slab % (2 * SUBLANES) == 0, (name, s.shape, steps)
        in_specs.append(pl.BlockSpec((None, slab, cols), lambda i, j, l=layer: (l, i * steps_n + j, 0)))
        out_specs.append(pl.BlockSpec((slab, cols), lambda i, j: (i * steps_n + j, 0)))
        out_shape.append(jax.ShapeDtypeStruct((rows, cols), BF16))
    n_in = len(in_specs) - len(side_casts)
    if side_casts:
        body = _with_side_casts(body, n_in, len(side_casts))
    res = pl.pallas_call(
        body,
        grid=(m // tm, steps_n),
        in_specs=in_specs,
        out_specs=out_specs,
        out_shape=out_shape,
        compiler_params=_params("arbitrary" if side_casts else "parallel", "arbitrary"),
        name=name,
    )(*a_ops, *b_ops, *mn_ops, *vec_ops, *[s for s, _ in side_casts])
    return tuple(res) if side_casts else res[0]


def _gelu_tanh(x):
    c = math.sqrt(2.0 / math.pi)
    return x * (0.5 * (1.0 + jnp.tanh(c * (x + 0.044715 * (x * x * x)))))


def _compress_kernel(x_ref, pos_ref, w1_ref, w2_ref, o_ref):
    t, dh = x_ref.shape[1], x_ref.shape[2]
    ns = t // CMP_STRIDE
    p1 = jnp.zeros((ns, dh), F32)
    p2 = jnp.zeros((ns, dh), F32)
    for i in range(CMP_STRIDE):
        xi = x_ref[0, pl.ds(i, ns, stride=CMP_STRIDE), :]
        p1 = p1 + _dot((xi + pos_ref[0, i:i + 1]).astype(BF16), w1_ref[0, i])
        p2 = p2 + _dot((xi + pos_ref[0, CMP_STRIDE + i:CMP_STRIDE + i + 1]).astype(BF16), w1_ref[0, CMP_STRIDE + i])
    pre = p1 + pltpu.roll(p2, ns - 1, 0)
    h = _gelu_tanh(pre).astype(BF16)
    o_ref[0, 0] = _dot(h, w2_ref[0]).astype(o_ref.dtype)


def _compress(kvc, pos, w1, w2):
    b, t, _ = kvc.shape
    dh = w2.shape[-1]
    g = N_KV_GROUPS
    ns = t // CMP_STRIDE
    return pl.pallas_call(
        _compress_kernel,
        grid=(2, b, g),
        in_specs=[pl.BlockSpec((1, t, dh), lambda k, bi, gi: (bi, 0, k * N_KV_GROUPS + gi)),
                  pl.BlockSpec((1, CMP_BLOCK, dh), lambda k, bi, gi: (k, 0, 0)),
                  pl.BlockSpec((1, CMP_BLOCK, dh, dh), lambda k, bi, gi: (k, 0, 0, 0)),
                  pl.BlockSpec((1, dh, dh), lambda k, bi, gi: (k, 0, 0))],
        out_specs=pl.BlockSpec((1, 1, ns, dh), lambda k, bi, gi: (k, bi * N_KV_GROUPS + gi, 0, 0)),
        out_shape=jax.ShapeDtypeStruct((2, b * g, ns, dh), BF16),
        compiler_params=_params("parallel", "parallel", "parallel"),
        name="nsa_compress",
    )(kvc, pos, w1, w2)


def _nt_dot(a, b):
    return lax.dot_general(a, b, (((1,), (1,)), ((), ())), preferred_element_type=F32)


def _nsa_kernel(q_ref, kslc_ref, vslc_ref, kwin_ref, vwin_ref, kcmp_ref, vcmp_ref, gates_ref,
                bnear_ref, bcmp_ref, emat_ref, selmap_t_ref, o_ref,
                m_ref, acc_ref):
    i = pl.program_id(2)
    tq = Q_TILE
    nh = HEADS_PER_GROUP
    dh = HEAD_DIM
    rows = nh * tq
    n_cmp = kcmp_ref.shape[2]

    def q_block(r0, hb):
        return jnp.concatenate([q_ref[0, :, r * dh:(r + 1) * dh] for r in range(r0, r0 + hb)], axis=0)

    def stack_heads(ref, r0, hb):
        return ref[r0:r0 + hb].reshape(hb * tq, ref.shape[-1])

    def block_rows(r0, hb):
        return slice(r0 * tq, (r0 + hb) * tq)

    def add_shared(s, shared):
        hb, n = s.shape[0] // tq, s.shape[1]
        return (s.reshape(hb, tq, n) + shared[None]).reshape(hb * tq, n)

    lane = lax.broadcasted_iota(jnp.int32, (tq, LANES), 1)

    nrow = lax.broadcasted_iota(jnp.int32, (n_cmp, LANES), 0)
    ncol = lax.broadcasted_iota(jnp.int32, (n_cmp, LANES), 1)
    place = jnp.where((nrow == (ncol & 15) + (8 * i - 8)) & (ncol < 48), 1.0, 0.0).astype(BF16)
    k_aug = jnp.concatenate([kcmp_ref[0, 0], place], axis=1)
    tc = lax.broadcasted_iota(jnp.int32, (tq, n_cmp), 0) + i * tq
    nc = lax.broadcasted_iota(jnp.int32, (tq, n_cmp), 1)
    ok_c = (tc - (nc * CMP_STRIDE + CMP_BLOCK - 1)) >= 0
    mask_c = jnp.where(ok_c, 0.0, NEG_INF)
    v_cmp = vcmp_ref[0, 0]
    o_cmp = []
    psum = jnp.zeros((tq, n_cmp), F32)
    for r0 in range(0, nh, CMP_HEADS):
        q_aug = jnp.concatenate([q_block(r0, CMP_HEADS), stack_heads(bcmp_ref, r0, CMP_HEADS)], axis=1)
        s = add_shared(_nt_dot(q_aug, k_aug), mask_c)
        e = jnp.exp2(s - jnp.max(s, axis=-1, keepdims=True))
        p = e / jnp.sum(e, axis=-1, keepdims=True)
        p3 = jnp.where(ok_c[None], p.reshape(CMP_HEADS, tq, n_cmp), 0.0)
        o_cmp.append(_dot(p3.reshape(CMP_HEADS * tq, n_cmp).astype(BF16), v_cmp))
        psum = psum + jnp.sum(p3, axis=0)
    o_cmp = jnp.concatenate(o_cmp, axis=0)

    p_hi = psum.astype(BF16)
    p_lo = (psum - p_hi.astype(F32)).astype(BF16)
    imp_t = _nt_dot(selmap_t_ref[...], p_hi) + _nt_dot(selmap_t_ref[...], p_lo)
    n_blocks = emat_ref.shape[1] // SEL_BLOCK
    assert n_blocks <= LANES and n_blocks % SUBLANES == 0
    blk = lax.broadcasted_iota(jnp.int32, (n_blocks, tq), 0)
    cur_t = (lax.broadcasted_iota(jnp.int32, (n_blocks, tq), 1) + i * tq) >> LOG2_SEL_BLOCK
    valid_t = blk <= cur_t
    forced_t = (blk == 0) | (blk == cur_t) | (blk == cur_t - 1)
    big = 1e30
    x = jnp.where(valid_t, jnp.where(forced_t, big, imp_t[:n_blocks]), -big)
    bcast = [jnp.broadcast_to(x[j:j + 1], (SUBLANES, tq)) for j in range(n_blocks)]
    cnt_groups = []
    for g0 in range(0, n_blocks, SUBLANES):
        xg = x[g0:g0 + SUBLANES]
        blk_g = blk[g0:g0 + SUBLANES]
        cnt = jnp.zeros((SUBLANES, tq), F32)
        for j in range(n_blocks):
            if j < g0:
                beats = bcast[j] >= xg
            elif j >= g0 + SUBLANES:
                beats = bcast[j] > xg
            else:
                beats = (bcast[j] > xg) | ((bcast[j] == xg) & (blk_g > j))
            cnt = cnt + jnp.where(beats, 1.0, 0.0)
        cnt_groups.append(cnt)
    cnt = jnp.concatenate(cnt_groups, axis=0)
    sel_t = jnp.where(valid_t & (cnt < SEL_TOPK), 1.0, 0.0)
    sel_t = jnp.concatenate([sel_t, jnp.zeros((LANES - n_blocks, tq), F32)], axis=0)
    sel = sel_t.T

    def lane_tile(x, n):
        return x if n == 1 else jnp.concatenate([x] * n, axis=1)

    def with_ones(v):
        return jnp.concatenate([v, jnp.ones_like(v)], axis=1)

    m_ref[...] = jnp.full(m_ref.shape, NEG_INF, F32)
    acc_ref[...] = jnp.zeros(acc_ref.shape, F32)

    def online_update(sl, s, v_ones):
        m_prev = m_ref[sl]
        m_next = jnp.maximum(m_prev, jnp.max(s, axis=-1, keepdims=True))
        alpha = jnp.exp2(m_prev - m_next)
        p = jnp.exp2(s - lane_tile(m_next, s.shape[-1] // LANES))
        acc_ref[sl] = lane_tile(alpha, 2) * acc_ref[sl] + _dot(p.astype(BF16), v_ones)
        m_ref[sl] = m_next

    sel_far = jnp.where(lane <= 2 * i - 3, sel, 0.0).astype(BF16)

    def far_body(c, carry):
        c0 = pl.multiple_of(c * FAR_CHUNK, FAR_CHUNK)
        k = kslc_ref[0, pl.ds(c0 + Q_TILE, FAR_CHUNK), :]
        v = with_ones(vslc_ref[0, pl.ds(c0 + Q_TILE, FAR_CHUNK), :])
        hit = _dot(sel_far, emat_ref[:, pl.ds(c0, FAR_CHUNK)])
        mask = (hit - 1.0) * (-NEG_INF)
        for h0 in range(0, nh, FAR_HEADS):
            online_update(block_rows(h0, FAR_HEADS), add_shared(_nt_dot(q_block(h0, FAR_HEADS), k), mask), v)
        return carry

    tiles_per_chunk = FAR_CHUNK // Q_TILE
    lax.fori_loop(0, (i + tiles_per_chunk - 2) // tiles_per_chunk, far_body, 0)

    r0 = pl.multiple_of(i * tq, tq)
    k = kslc_ref[0, pl.ds(r0, NEAR_KEYS), :]
    v = with_ones(vslc_ref[0, pl.ds(r0, NEAR_KEYS), :])
    erow = lax.broadcasted_iota(jnp.int32, (LANES, NEAR_KEYS), 0)
    ecol = lax.broadcasted_iota(jnp.int32, (LANES, NEAR_KEYS), 1)
    e_near = jnp.where(erow == 2 * i - 2 + (ecol >> LOG2_SEL_BLOCK), 1.0, 0.0).astype(BF16)
    hit = _dot(sel.astype(BF16), e_near)
    mask = (hit - 1.0) * (-NEG_INF)
    for h0 in range(0, nh, NEAR_HEADS):
        s = add_shared(_nt_dot(q_block(h0, NEAR_HEADS), k), mask) + stack_heads(bnear_ref, h0, NEAR_HEADS)
        online_update(block_rows(h0, NEAR_HEADS), s, v)

    k_w = kwin_ref[0, pl.ds(r0, WIN_KEYS), :]
    v_w = with_ones(vwin_ref[0, pl.ds(r0, WIN_KEYS), :])
    wrow = lax.broadcasted_iota(jnp.int32, (tq, WINDOW), 0)
    wcol = lax.broadcasted_iota(jnp.int32, (tq, WINDOW), 1)
    ok_w = (wcol >= WINDOW - i * tq) & ((wcol >= tq) | (wcol > wrow))
    mask_w = jnp.where(ok_w, 0.0, NEG_INF)
    old = WIN_KEYS - NEAR_KEYS
    gates = jax.nn.sigmoid(gates_ref[0, 0])
    for h0 in range(0, nh, WIN_HEADS):
        s = _nt_dot(q_block(h0, WIN_HEADS), k_w)
        bias = stack_heads(bnear_ref, h0, WIN_HEADS)
        s = jnp.concatenate([add_shared(s[:, :old], mask_w[:, :old]),
                             add_shared(s[:, old:WINDOW], mask_w[:, old:]) + bias[:, :WINDOW - old],
                             s[:, WINDOW:] + bias[:, WINDOW - old:]], axis=1)
        pv = _dot(jnp.exp2(s - jnp.max(s, axis=-1, keepdims=True)).astype(BF16), v_w)
        o_win = pv[:, :dh] / pv[:, dh:]
        acc = acc_ref[block_rows(h0, WIN_HEADS)]
        o_slc = acc[:, :dh] / acc[:, dh:]
        for r in range(h0, h0 + WIN_HEADS):
            sl = block_rows(r - h0, 1)
            o = (gates[:, 3 * r:3 * r + 1] * o_cmp[block_rows(r, 1)]
                 + gates[:, 3 * r + 1:3 * r + 2] * o_slc[sl]
                 + gates[:, 3 * r + 2:3 * r + 3] * o_win[sl])
            o_ref[0, :, r * dh:(r + 1) * dh] = o.astype(o_ref.dtype)


def _nsa_attention(q, kslc, vslc, kwin, vwin, kvcmp, gates, bnear, bcmp, emat, selmap_t):
    b, t, _ = q.shape
    g = N_KV_GROUPS
    nh = HEADS_PER_GROUP
    dh = HEAD_DIM
    tq = Q_TILE
    n_cmp = kvcmp.shape[2]
    tp_slc = kslc.shape[1]
    tp_win = kwin.shape[1]
    rows = nh * tq
    kv_spec = lambda tp, w: pl.BlockSpec((1, tp, w), lambda bi, gi, ti: (bi, 0, gi))
    return pl.pallas_call(
        _nsa_kernel,
        grid=(b, g, t // tq),
        in_specs=[
            pl.BlockSpec((1, tq, nh * dh), lambda bi, gi, ti: (bi, ti, gi)),
            kv_spec(tp_slc, dh), kv_spec(tp_slc, dh), kv_spec(tp_win, dh), kv_spec(tp_win, dh),
            pl.BlockSpec((1, 1, n_cmp, dh), lambda bi, gi, ti: (0, bi * N_KV_GROUPS + gi, 0, 0)),
            pl.BlockSpec((1, 1, n_cmp, dh), lambda bi, gi, ti: (1, bi * N_KV_GROUPS + gi, 0, 0)),
            pl.BlockSpec((1, 1, tq, LANES), lambda bi, gi, ti: (bi, gi, ti, 0)),
            pl.BlockSpec((nh, tq, NEAR_KEYS), lambda bi, gi, ti: (gi, 0, 0)),
            pl.BlockSpec((nh, tq, LANES), lambda bi, gi, ti: (gi, 0, 0)),
            pl.BlockSpec(emat.shape, lambda bi, gi, ti: (0, 0)),
            pl.BlockSpec(selmap_t.shape, lambda bi, gi, ti: (0, 0)),
        ],
        out_specs=pl.BlockSpec((1, tq, nh * dh), lambda bi, gi, ti: (bi, ti, gi)),
        out_shape=jax.ShapeDtypeStruct((b, t, g * nh * dh), BF16),
        scratch_shapes=[pltpu.VMEM((rows, LANES), F32),
                        pltpu.VMEM((rows, 2 * dh), F32)],
        compiler_params=_params("parallel", "parallel", "arbitrary"),
        name="nsa_attention",
    )(q, kslc, vslc, kwin, vwin, kvcmp, kvcmp, gates, bnear, bcmp, emat, selmap_t)


def _t5_bucket_table(max_dist):
    d = np.arange(max_dist)
    nf = np.maximum(d, MAX_EXACT).astype(np.float32)
    large = MAX_EXACT + (np.log(nf / np.float32(MAX_EXACT)) / np.float32(math.log(MAX_DISTANCE / MAX_EXACT))
                         * np.float32(NUM_BUCKETS - MAX_EXACT)).astype(np.int32)
    large = np.minimum(large, NUM_BUCKETS - 1)
    return np.where(d < MAX_EXACT, d, large).astype(np.int32)


def _nsa_constants(rel_bias, seq):
    tq = Q_TILE
    max_d = NEAR_KEYS
    bucket = _t5_bucket_table(max_d)
    far_cmp = tq - (CMP_BLOCK - 1) + CMP_STRIDE
    assert np.all(bucket[min(far_cmp, tq + 1):] == NUM_BUCKETS - 1)
    table = rel_bias.astype(F32).T * LOG2_E
    delta = table[:, bucket] - table[:, NUM_BUCKETS - 1:]

    strip = jnp.concatenate([delta[:, ::-1], jnp.full((N_HEADS, tq - 1), NEG_INF, F32)], axis=1)
    bnear = jnp.stack([strip[:, tq - 1 - t:tq - 1 - t + NEAR_KEYS] for t in range(tq)], axis=1)
    t = np.arange(tq)[:, None]
    npr = np.arange(16)[None, :]
    d_cmp = t - CMP_STRIDE * npr + (8 * CMP_STRIDE - (CMP_BLOCK - 1))
    bc = jnp.where(jnp.asarray(d_cmp >= 0)[None], delta[:, np.clip(d_cmp, 0, max_d - 1)], 0.0)
    hi = bc.astype(BF16)
    mid = (bc - hi.astype(F32)).astype(BF16)
    lo = (bc - hi.astype(F32) - mid.astype(F32)).astype(BF16)
    bcmp = jnp.concatenate([hi, mid, lo, jnp.zeros((N_HEADS, tq, LANES - 48), BF16)], axis=-1)

    nsb = seq // SEL_BLOCK
    emat = np.zeros((LANES, seq), np.float32)
    emat[np.arange(seq) // SEL_BLOCK, np.arange(seq)] = 1.0
    ns = seq // CMP_STRIDE
    c_start = np.arange(ns) * CMP_STRIDE
    s_start = np.arange(nsb) * SEL_BLOCK
    ov = np.minimum(c_start[:, None] + CMP_BLOCK, s_start[None, :] + SEL_BLOCK) - np.maximum(c_start[:, None], s_start[None, :])
    ov = np.clip(ov, 0, None) // CMP_STRIDE
    selmap_t = np.zeros((LANES, ns), np.float32)
    selmap_t[:nsb, :ns - 1] = ov[:ns - 1].T
    return bnear, bcmp, jnp.asarray(emat, BF16), jnp.asarray(selmap_t, BF16)


def _conv_kernel(cur_ref, prev_ref, w_ref, b_ref, g_ref, beta_ref, o_ref, shift_ref, conv_ref):
    tt, c = cur_ref.shape[1], cur_ref.shape[2]
    first = pl.program_id(1) == 0
    base = CONV_HALO - (CONV_WIDTH - 1)
    groups = CONV_ROW_BLOCK // SUBLANES

    def chunk_body(ci, carry):
        lanes = pl.ds(pl.multiple_of(ci * CONV_LANE_CHUNK, CONV_LANE_CHUNK), CONV_LANE_CHUNK)
        halo = jnp.where(first, 0.0, prev_ref[0, :, lanes])
        xin = jnp.concatenate([halo, cur_ref[0, :, lanes]], axis=0)
        shift_ref[0] = xin
        keep = tt + CONV_HALO - SUBLANES
        for s in range(1, SUBLANES):
            shift_ref[s, 0:keep, :] = xin[s:s + keep]
        bias = b_ref[:, lanes]
        for r0 in range(0, tt, CONV_ROW_BLOCK):
            acc = jnp.zeros((groups, SUBLANES, CONV_LANE_CHUNK), F32) + bias
            for k in range(CONV_WIDTH):
                s = (base + k) % SUBLANES
                a = r0 + base + k - s
                win = shift_ref[s, a:a + CONV_ROW_BLOCK, :].reshape(groups, SUBLANES, CONV_LANE_CHUNK)
                acc = acc + w_ref[k, :, lanes][None] * win
            conv_ref[r0:r0 + CONV_ROW_BLOCK, lanes] = acc.reshape(CONV_ROW_BLOCK, CONV_LANE_CHUNK)
        return carry

    lax.fori_loop(0, c // CONV_LANE_CHUNK, chunk_body, 0)

    acc = conv_ref[...]
    mu = jnp.mean(acc, axis=-1, keepdims=True)
    xc = acc - mu
    var = jnp.mean(xc * xc, axis=-1, keepdims=True)
    y = xc * lax.rsqrt(var + EPS) * g_ref[...] + beta_ref[...]
    o_ref[0] = jax.nn.silu(y).astype(o_ref.dtype)


def _conformer_conv(h, w_dw, b_dw, ln_g, ln_b, tt=256):
    b, t, c = h.shape
    tt = min(tt, t)
    hb = tt // CONV_HALO
    assert c % CONV_LANE_CHUNK == 0 and tt % CONV_ROW_BLOCK == 0 and t % tt == 0
    wrep = jnp.broadcast_to(w_dw.astype(F32)[:, None, :], (CONV_WIDTH, SUBLANES, c))
    vec = lambda a: a.reshape(1, c).astype(F32)
    return pl.pallas_call(
        _conv_kernel,
        grid=(b, t // tt),
        in_specs=[pl.BlockSpec((1, tt, c), lambda bi, ti: (bi, ti, 0)),
                  pl.BlockSpec((1, CONV_HALO, c), lambda bi, ti: (bi, jnp.maximum(ti * hb - 1, 0), 0)),
                  pl.BlockSpec((CONV_WIDTH, SUBLANES, c), lambda bi, ti: (0, 0, 0)),
                  pl.BlockSpec((1, c), lambda bi, ti: (0, 0)),
                  pl.BlockSpec((1, c), lambda bi, ti: (0, 0)),
                  pl.BlockSpec((1, c), lambda bi, ti: (0, 0))],
        out_specs=pl.BlockSpec((1, tt, c), lambda bi, ti: (bi, ti, 0)),
        out_shape=jax.ShapeDtypeStruct((b, t, c), BF16),
        scratch_shapes=[pltpu.VMEM((SUBLANES, CONV_HALO + tt, CONV_LANE_CHUNK), F32),
                        pltpu.VMEM((tt, c), F32)],
        compiler_params=_params("parallel", "arbitrary"),
        name="conformer_conv",
    )(h, h, wrep, vec(b_dw), vec(ln_g), vec(ln_b))


def _nsa(h, w_in_bf, layer, w_gates_l, w_attn_out_side, cmp_pos, cmp_w1, cmp_w2, consts, batch, seq):
    dh = HEAD_DIM
    g = N_KV_GROUPS
    attn_w = N_HEADS * dh
    kv_w = g * dh
    m = h.shape[0]
    c0 = attn_w
    plain = functools.partial(_mm_scale_kernel, scale=1.0)
    q, w_attn_bf = _mm_call(functools.partial(_mm_scale_kernel, scale=dh ** -0.5 * LOG2_E), [h],
                            [_panel(w_in_bf, layer)], [], [], BF16, "proj_q", n=attn_w, side_casts=[w_attn_out_side])
    kvc = _mm_call(plain, [h], [_panel(w_in_bf, layer, c0)], [], [], F32, "proj_kv_cmp", n=2 * kv_w)
    kvr = _mm_call(plain, [h], [_panel(w_in_bf, layer, c0 + 2 * kv_w)], [], [], BF16, "proj_kv", n=4 * kv_w)
    n_gates = 3 * N_HEADS
    w_br = jnp.pad(w_gates_l, ((0, 0), (0, LANES - n_gates))).astype(BF16)
    br = _mm_call(plain, [h], [_panel(w_br)], [], [], F32, "proj_branch_gates")

    w1 = cmp_w1.astype(BF16).reshape(2, CMP_BLOCK, dh, dh)
    kvcmp = _compress(kvc.reshape(batch, seq, 2 * kv_w), cmp_pos.astype(F32), w1, cmp_w2.astype(BF16))

    kvr = kvr.reshape(batch, seq, 4 * kv_w)
    padt = lambda a, n: jnp.pad(a, ((0, 0), (n, 0), (0, 0)))
    kslc = padt(kvr[:, :, 0:kv_w], Q_TILE)
    vslc = padt(kvr[:, :, kv_w:2 * kv_w], Q_TILE)
    kwin = padt(kvr[:, :, 2 * kv_w:3 * kv_w], WINDOW)
    vwin = padt(kvr[:, :, 3 * kv_w:4 * kv_w], WINDOW)
    per_g = 3 * HEADS_PER_GROUP
    gates = br[:, :n_gates].reshape(batch, seq, g, per_g).transpose(0, 2, 1, 3)
    gates = jnp.pad(gates, ((0, 0), (0, 0), (0, 0), (0, LANES - per_g)))
    o = _nsa_attention(q.reshape(batch, seq, attn_w), kslc, vslc, kwin, vwin, kvcmp, gates, *consts)
    return o.reshape(m, attn_w), w_attn_bf


def kernel(x, w_in, cmp_pos, cmp_w1, cmp_w2, rel_bias, w_attn_out, b_glu, w_dw, b_dw, conv_ln_g, conv_ln_b, w_conv_out, b_conv_out, w_out, norm_mix, norm_ffn, w_ffn_gate, w_ffn_up, w_ffn_down, norm_final):
    batch, seq, d = x.shape
    depth = w_in.shape[0]
    m = batch * seq
    attn_w = N_HEADS * HEAD_DIM
    kv_w = N_KV_GROUPS * HEAD_DIM
    conv_ch = w_dw.shape[-1]
    c_conv = attn_w + 6 * kv_w
    c_gate = c_conv + 2 * conv_ch
    consts = _nsa_constants(rel_bias, seq)
    xf = x.reshape(m, d).astype(F32)
    row = lambda a: a.reshape(1, -1).astype(F32)
    w_in_bf = w_in.astype(BF16)
    for l in range(depth):
        h = _rmsnorm(xf, norm_mix[l], BF16)
        attn, w_attn_bf = _nsa(h, w_in_bf, l, w_in[l, :, w_in.shape[-1] - 3 * N_HEADS:], (w_attn_out, l),
                               cmp_pos[l], cmp_w1[l], cmp_w2[l], consts, batch, seq)
        glu, w_conv_bf = _mm_call(_mm_glu_kernel, [h], [_panel(w_in_bf, l, c_conv), _panel(w_in_bf, l, c_conv + conv_ch)],
                                  [], [row(b_glu[l, :conv_ch]), row(b_glu[l, conv_ch:])], F32, "proj_conv_glu",
                                  n=conv_ch, side_casts=[(w_conv_out, l)])
        conv = _conformer_conv(glu.reshape(batch, seq, conv_ch), w_dw[l], b_dw[l], conv_ln_g[l], conv_ln_b[l])
        gate_ab, w_out_bf = _mm_call(functools.partial(_mm_scale_kernel, scale=1.0), [h], [_panel(w_in_bf, l, c_gate)],
                                     [], [], F32, "proj_merge_gates", n=2 * d, side_casts=[(w_out, l)])
        merged, w_gate_bf = _mm_call(_mm_merge_kernel, [attn, conv.reshape(m, conv_ch)],
                                     [_panel(w_attn_bf), _panel(w_conv_bf)],
                                     [(gate_ab, 0), (gate_ab, d)], [row(b_conv_out[l])], BF16, "merge",
                                     side_casts=[(w_ffn_gate, l)])
        xf, w_up_bf = _mm_call(_mm_residual_kernel, [merged], [_panel(w_out_bf)], [(xf, 0)], [], F32, "mix_out",
                               side_casts=[(w_ffn_up, l)])
        h = _rmsnorm(xf, norm_ffn[l], BF16)
        gu, w_down_bf = _mm_call(_mm_swiglu_kernel, [h], [_panel(w_gate_bf), _panel(w_up_bf)], [], [], BF16,
                                 "ffn_swiglu", side_casts=[(w_ffn_down, l)])
        xf = _mm_call(_mm_residual_kernel, [gu], [_panel(w_down_bf)], [(xf, 0)], [], F32, "ffn_down")
    out = _rmsnorm(xf, norm_final, F32)
    return out.reshape(batch, seq, d).astype(x.dtype)
```
